```python
import jax, jax.numpy as jnp
from jax import lax
import numpy as np

D_MODEL = 2048
BATCH = 8
SEQ = 2048
DEPTH = 1

GRID_W = 64
CTX_LEN = 256
D_MIX = D_MODEL
D_FOURIER = D_MIX // 2
N_FOURIER_GROUPS = 4
FOURIER_GROUP = D_FOURIER // N_FOURIER_GROUPS
D_LRU = D_MIX - D_FOURIER
N_LRU_HEADS = 8
LRU_HEAD = D_LRU // N_LRU_HEADS
LRU_CONV = 4
LRU_C = 8.0
D_FF = 5632
FFN_CONV = 3
N_MOD = 6
EPS = 1e-6
POS_BASE = 10000.0

kernel_name = "hybrid_fourier_rglru_dit_block"


def rms_norm(x, g):
    xf = x.astype(jnp.float32)
    y = xf * lax.rsqrt(jnp.mean(xf * xf, axis=-1, keepdims=True) + EPS)
    return (y * g.astype(jnp.float32)).astype(x.dtype)


def depthwise_conv(x, w, b, left):
    k_w = w.shape[0]
    n = x.shape[1]
    xp = jnp.pad(x, ((0, 0), (left, k_w - 1 - left), (0, 0)))
    y = b
    for k in range(k_w):
        y = y + xp[:, k:k + n] * w[k]
    return y


def grid_pos_embed(n_tokens, dtype):
    rows = n_tokens // GRID_W
    row = jnp.repeat(jnp.arange(rows, dtype=jnp.float32), GRID_W)
    col = jnp.tile(jnp.arange(GRID_W, dtype=jnp.float32), rows)
    quarter = D_MODEL // 4
    freqs = POS_BASE ** (-jnp.arange(quarter, dtype=jnp.float32) / quarter)

    def enc(p):
        ang = p[:, None] * freqs[None, :]
        return jnp.concatenate([jnp.sin(ang), jnp.cos(ang)], axis=-1)

    return jnp.concatenate([enc(row), enc(col)], axis=-1).astype(dtype)


def modulation(cond, w_ada, b_ada):
    m = jax.nn.silu(cond) @ w_ada + b_ada
    return [t[:, None, :] for t in jnp.split(m, N_MOD, axis=-1)]


def modulate(h, shift, scale):
    return h * (1 + scale) + shift


def mixer_inputs(h, w_in, conv_w, conv_b):
    proj = h @ w_in
    u_f, u_x, u_g = jnp.split(proj, [D_FOURIER, D_FOURIER + D_LRU], axis=-1)
    x_c = depthwise_conv(u_x, conv_w, conv_b, LRU_CONV // 2)
    return u_f, x_c, u_g


def lru_coeffs(x_c, w_a, b_a, w_x, b_x, lam):
    bsz, n, _ = x_c.shape
    xh = x_c.reshape(bsz, n, N_LRU_HEADS, LRU_HEAD)
    r = jax.nn.sigmoid((jnp.einsum('blhi,hij->blhj', xh, w_a).reshape(bsz, n, D_LRU) + b_a).astype(jnp.float32))
    i = jax.nn.sigmoid((jnp.einsum('blhi,hij->blhj', xh, w_x).reshape(bsz, n, D_LRU) + b_x).astype(jnp.float32))
    log_a = LRU_C * r * jax.nn.log_sigmoid(lam.astype(jnp.float32))
    a = jnp.exp(log_a)
    inp = jnp.sqrt(-jnp.expm1(2.0 * log_a)) * i * x_c.astype(jnp.float32)
    return a, inp


def linear_scan(a, b, h0, reverse):
    if h0 is not None:
        if reverse:
            b = b.at[:, -1].add(a[:, -1] * h0)
        else:
            b = b.at[:, 0].add(a[:, 0] * h0)

    def combine(e1, e2):
        a1, b1 = e1
        a2, b2 = e2
        return a1 * a2, a2 * b1 + b2

    _, h = lax.associative_scan(combine, (a, b), reverse=reverse, axis=1)
    return h


def bi_rglru(x_c, w_a, b_a, w_x, b_x, lam, h0_fwd, h0_bwd):
    a_f, i_f = lru_coeffs(x_c, w_a[0], b_a[0], w_x[0], b_x[0], lam[0])
    h_f = linear_scan(a_f, i_f, h0_fwd, reverse=False)
    a_b, i_b = lru_coeffs(x_c, w_a[1], b_a[1], w_x[1], b_x[1], lam[1])
    h_b = linear_scan(a_b, i_b, h0_bwd, reverse=True)
    return h_f, h_b


def fourier_mix(u_f, w_f, b_f):
    bsz, n, _ = u_f.shape
    ug = u_f.reshape(bsz, n, N_FOURIER_GROUPS, FOURIER_GROUP).astype(jnp.float32)
    f = jnp.fft.fft2(ug, axes=(1, 3), norm="ortho").real.astype(u_f.dtype)
    y = jnp.einsum('blgc,gcd->blgd', f, w_f).reshape(bsz, n, D_FOURIER)
    return y + b_f


def mixer_output(u_f, h_f, h_b, u_g, w_f, b_f, w_out):
    y_fourier = fourier_mix(u_f, w_f, b_f)
    y_lru = (h_f + h_b).astype(u_g.dtype) * jax.nn.gelu(u_g, approximate=True)
    return jnp.concatenate([y_fourier, y_lru], axis=-1) @ w_out


def conv_ffn(h, w_up, conv_w, conv_b, w_down):
    up = depthwise_conv(h @ w_up, conv_w, conv_b, FFN_CONV // 2)
    g, v = jnp.split(up, 2, axis=-1)
    return (jax.nn.gelu(g, approximate=True) * v) @ w_down


def setup_inputs(seed: int = 0) -> dict:
    key = jax.random.key(seed)
    ks = jax.random.split(key, 32)
    f32 = jnp.float32
    nrm = lambda k, shape, s: jax.random.normal(k, shape, f32) * s
    u = jax.random.uniform(ks[16], (DEPTH, 2, D_LRU), f32, 0.9, 0.999)
    a_base = u ** (1.0 / LRU_C)
    lam = jnp.log(a_base) - jnp.log1p(-a_base)
    return {
        "x": nrm(ks[0], (BATCH, SEQ, D_MODEL), 1.0),
        "c": nrm(ks[1], (BATCH, D_MODEL), 1.0),
        "ctx": nrm(ks[2], (BATCH, CTX_LEN, D_MODEL), 1.0),
        "c_ctx": nrm(ks[3], (D_MODEL,), 1.0),
        "w_ada": nrm(ks[4], (DEPTH, D_MODEL, N_MOD * D_MODEL), 0.5 * D_MODEL ** -0.5),
        "b_ada": nrm(ks[5], (DEPTH, N_MOD * D_MODEL), 0.01),
        "g_mix_pre": 1.0 + nrm(ks[6], (DEPTH, D_MODEL), 0.02),
        "g_mix_post": 1.0 + nrm(ks[7], (DEPTH, D_MODEL), 0.02),
        "g_ffn_pre": 1.0 + nrm(ks[8], (DEPTH, D_MODEL), 0.02),
        "g_ffn_post": 1.0 + nrm(ks[9], (DEPTH, D_MODEL), 0.02),
        "w_in": nrm(ks[10], (DEPTH, D_MODEL, D_FOURIER + 2 * D_LRU), D_MODEL ** -0.5),
        "conv_lru_w": nrm(ks[11], (DEPTH, LRU_CONV, D_LRU), LRU_CONV ** -0.5),
        "conv_lru_b": nrm(ks[12], (DEPTH, D_LRU), 0.01),
        "w_rec_gate": nrm(ks[13], (DEPTH, 2, N_LRU_HEADS, LRU_HEAD, LRU_HEAD), LRU_HEAD ** -0.5),
        "b_rec_gate": nrm(ks[14], (DEPTH, 2, D_LRU), 0.01),
        "w_in_gate": nrm(ks[15], (DEPTH, 2, N_LRU_HEADS, LRU_HEAD, LRU_HEAD), LRU_HEAD ** -0.5),
        "b_in_gate": nrm(ks[17], (DEPTH, 2, D_LRU), 0.01),
        "lru_lambda": lam,
        "w_fourier": nrm(ks[18], (DEPTH, N_FOURIER_GROUPS, FOURIER_GROUP, FOURIER_GROUP), FOURIER_GROUP ** -0.5),
        "b_fourier": nrm(ks[19], (DEPTH, D_FOURIER), 0.01),
        "w_out": nrm(ks[20], (DEPTH, D_MIX, D_MODEL), D_MIX ** -0.5),
        "w_up": nrm(ks[21], (DEPTH, D_MODEL, 2 * D_FF), D_MODEL ** -0.5),
        "conv_ffn_w": nrm(ks[22], (DEPTH, FFN_CONV, 2 * D_FF), FFN_CONV ** -0.5),
        "conv_ffn_b": nrm(ks[23], (DEPTH, 2 * D_FF), 0.01),
        "w_down": nrm(ks[24], (DEPTH, D_FF, D_MODEL), D_FF ** -0.5),
    }


def reference(x, c, ctx, c_ctx, w_ada, b_ada, g_mix_pre, g_mix_post, g_ffn_pre, g_ffn_post,
              w_in, conv_lru_w, conv_lru_b, w_rec_gate, b_rec_gate, w_in_gate, b_in_gate,
              lru_lambda, w_fourier, b_fourier, w_out, w_up, conv_ffn_w, conv_ffn_b, w_down):
    n_lat = x.shape[1]
    x = x + grid_pos_embed(n_lat, x.dtype)[None]
    for l in range(DEPTH):
        last = l == DEPTH - 1
        sh1, sc1, gt1, sh2, sc2, gt2 = modulation(c, w_ada[l], b_ada[l])
        csh1, csc1, cgt1, csh2, csc2, cgt2 = modulation(c_ctx[None], w_ada[l], b_ada[l])
        lru_p = (w_rec_gate[l], b_rec_gate[l], w_in_gate[l], b_in_gate[l], lru_lambda[l])

        hc = modulate(rms_norm(ctx, g_mix_pre[l]), csh1, csc1)
        uf_c, xc_c, ug_c = mixer_inputs(hc, w_in[l], conv_lru_w[l], conv_lru_b[l])
        hf_c, hb_c = bi_rglru(xc_c, *lru_p, None, None)

        hx = modulate(rms_norm(x, g_mix_pre[l]), sh1, sc1)
        uf_x, xc_x, ug_x = mixer_inputs(hx, w_in[l], conv_lru_w[l], conv_lru_b[l])
        hf_x, hb_x = bi_rglru(xc_x, *lru_p, hf_c[:, -1], hb_c[:, 0])
        y_x = mixer_output(uf_x, hf_x, hb_x, ug_x, w_fourier[l], b_fourier[l], w_out[l])
        x = x + gt1 * rms_norm(y_x, g_mix_post[l])

        hx2 = modulate(rms_norm(x, g_ffn_pre[l]), sh2, sc2)
        x = x + gt2 * rms_norm(conv_ffn(hx2, w_up[l], conv_ffn_w[l], conv_ffn_b[l], w_down[l]), g_ffn_post[l])

        if not last:
            y_c = mixer_output(uf_c, hf_c, hb_c, ug_c, w_fourier[l], b_fourier[l], w_out[l])
            ctx = ctx + cgt1 * rms_norm(y_c, g_mix_post[l])
            hc2 = modulate(rms_norm(ctx, g_ffn_pre[l]), csh2, csc2)
            ctx = ctx + cgt2 * rms_norm(conv_ffn(hc2, w_up[l], conv_ffn_w[l], conv_ffn_b[l], w_down[l]), g_ffn_post[l])
    return x
```

```python
import functools
import math

import jax
import jax.numpy as jnp
from jax import lax
from jax.experimental import pallas as pl
from jax.experimental.pallas import tpu as pltpu

F32 = jnp.float32
BF16 = jnp.bfloat16

EPS = 1e-6
POS_BASE = 10000.0
GRID_W = 64
N_MOD = 6
N_FOURIER_GROUPS = 4
N_LRU_HEADS = 8
LRU_CONV = 4
LRU_C = 8.0
FFN_CONV = 3

LANES = 128
SUBLANES = 8
VMEM_LIMIT_BYTES = 60000 * 1024

MOD_ROWS = 16
IN_TM = 512
LRU_CB = 256
LRU_ROW_CHUNK = 512
FFN_TM = 512
FFN_TF = 512
FFN_HALO = 16


def _cparams(sem):
    return pltpu.CompilerParams(dimension_semantics=sem, vmem_limit_bytes=VMEM_LIMIT_BYTES)


def _const_spec(shape):
    nd = len(shape)
    return pl.BlockSpec(shape, lambda *_: (0,) * nd, pipeline_mode=pl.Buffered(1))


def _mod_kernel(c_ref, w_ref, b_ref, o_ref):
    cv = c_ref[...]
    s = (cv * jax.nn.sigmoid(cv)).astype(BF16)
    o_ref[...] = jnp.dot(s, w_ref[...].astype(BF16), preferred_element_type=F32) + b_ref[...]


def _modulation(cond, w_ada, b_ada):
    d = cond.shape[1]
    n = w_ada.shape[1]
    tn = 1024
    return pl.pallas_call(
        _mod_kernel,
        grid=(n // tn,),
        in_specs=[
            pl.BlockSpec((MOD_ROWS, d), lambda j: (0, 0)),
            pl.BlockSpec((d, tn), lambda j: (0, j)),
            pl.BlockSpec((1, tn), lambda j: (0, j)),
        ],
        out_specs=pl.BlockSpec((MOD_ROWS, tn), lambda j: (0, j)),
        out_shape=jax.ShapeDtypeStruct((MOD_ROWS, n), F32),
        compiler_params=_cparams(("parallel",)),
        name="modulation",
    )(cond, w_ada, b_ada.reshape(1, n))


def _rms_norm(v, g):
    ms = jnp.mean(v * v, axis=-1, keepdims=True)
    return v * lax.rsqrt(ms + EPS) * g


def _in_proj_kernel(*refs, has_pe, col_splits):
    if has_pe:
        x_ref, pe_ref, mod_ref, g_ref, w_ref = refs[:5]
        out_refs = refs[5:]
        xp = x_ref[0] + pe_ref[...]
    else:
        x_ref, mod_ref, g_ref, w_ref = refs[:4]
        out_refs = refs[4:]
        xp = x_ref[0]
    h = _rms_norm(xp, g_ref[...]) * (1.0 + mod_ref[0, 1:2, :]) + mod_ref[0, 0:1, :]
    hb = h.astype(BF16)
    for o_ref, (c0, c1) in zip(out_refs, col_splits):
        o_ref[0] = jnp.dot(hb, w_ref[:, c0:c1], preferred_element_type=F32).astype(o_ref.dtype)


def _in_proj(x, pe, mod3, mod_row, g, w, col_splits, out_dtypes, tm, name):
    bsz, n, d = x.shape
    has_pe = pe is not None
    in_specs = [pl.BlockSpec((1, tm, d), lambda i, b: (b, i, 0))]
    args = [x]
    if has_pe:
        in_specs.append(pl.BlockSpec((tm, d), lambda i, b: (i, 0)))
        args.append(pe)
    in_specs += [
        pl.BlockSpec((1, N_MOD, d), lambda i, b: (mod_row(b), 0, 0)),
        _const_spec((1, d)),
        _const_spec(w.shape),
    ]
    args += [mod3, g.reshape(1, d), w]
    out_specs = [pl.BlockSpec((1, tm, c1 - c0), lambda i, b: (b, i, 0)) for c0, c1 in col_splits]
    out_shape = [jax.ShapeDtypeStruct((bsz, n, c1 - c0), dt) for (c0, c1), dt in zip(col_splits, out_dtypes)]
    return pl.pallas_call(
        functools.partial(_in_proj_kernel, has_pe=has_pe, col_splits=col_splits),
        grid=(n // tm, bsz),
        in_specs=in_specs,
        out_specs=out_specs,
        out_shape=out_shape,
        compiler_params=_cparams(("parallel", "parallel")),
        name=name,
    )(*args)


def _scan_pitch(n):
    m = -(-n // SUBLANES)
    while m % 8 != 4:
        m += 1
    return m


def _depthwise_conv_rows(x, w_ref, b_ref, left):
    n = x.shape[0]
    k_w = w_ref.shape[0]
    row = lax.broadcasted_iota(jnp.int32, x.shape, 0)
    y = b_ref[...] + jnp.zeros_like(x)
    for k in range(k_w):
        off = k - left
        if off == 0:
            xs = x
        elif off < 0:
            xs = jnp.where(row >= -off, pltpu.roll(x, -off, 0), 0.0)
        else:
            xs = jnp.where(row < n - off, pltpu.roll(x, n - off, 0), 0.0)
        y = y + xs * w_ref[k:k + 1, :]
    return y


def _sigmoid(z):
    return 0.5 + 0.5 * jnp.tanh(0.5 * z)


def _log_sigmoid(z):
    return jnp.minimum(z, 0.0) - jnp.log1p(jnp.exp(-jnp.abs(z)))


def _lru_gates(xc_ref, n, row_chunk, wg_ref, bg_ref, ls, a_ref, b_ref, n_slabs):
    for r0 in range(0, n, row_chunk):
        for s in range(n_slabs):
            lanes = slice(s * LANES, (s + 1) * LANES)
            xh = xc_ref[r0:r0 + row_chunk, lanes]
            g = jnp.dot(xh.astype(BF16), wg_ref[s], preferred_element_type=F32)
            for d in range(2):
                za = g[:, (2 * d) * LANES:(2 * d + 1) * LANES] + bg_ref[2 * d:2 * d + 1, lanes]
                zx = g[:, (2 * d + 1) * LANES:(2 * d + 2) * LANES] + bg_ref[2 * d + 1:2 * d + 2, lanes]
                r = _sigmoid(za)
                i = _sigmoid(zx)
                log_a = (LRU_C * r) * ls[d:d + 1, lanes]
                t = jnp.tanh(0.5 * log_a)
                q = 1.0 / (1.0 - t)
                a_ref[d, s, r0:r0 + row_chunk, :] = (1.0 + t) * q
                b_ref[d, s, r0:r0 + row_chunk, :] = (2.0 * q) * jnp.sqrt(-t) * i * xh


def _lru_scan(a_ref, b_ref, sh_ref, sp_ref, n, h0, n_slabs, write_states):
    m = _scan_pitch(n)
    pairs = [(d, s) for d in range(2) for s in range(n_slabs)]
    sub = lax.broadcasted_iota(jnp.int32, (SUBLANES, LANES), 0)

    def row_of(d, j):
        return j if d == 0 else m - 1 - j

    def local_body(j, carry):
        out = []
        for (d, s), (h, p) in zip(pairs, carry):
            jj = row_of(d, j)
            a = a_ref[d, s, pl.ds(jj, SUBLANES, stride=m), :]
            b = b_ref[d, s, pl.ds(jj, SUBLANES, stride=m), :]
            h = a * h + b
            p = a * p
            if write_states:
                dst = pl.ds(pl.multiple_of(jj * SUBLANES, SUBLANES), SUBLANES)
                sh_ref[d, s, dst, :] = h
                sp_ref[d, s, dst, :] = p
            out.append((h, p))
        return tuple(out)

    init = tuple((jnp.zeros((SUBLANES, LANES), F32), jnp.ones((SUBLANES, LANES), F32)) for _ in pairs)
    local = lax.fori_loop(0, m, local_body, init, unroll=2)

    carries, finals = [], []
    for (d, s), (hfin, ptot) in zip(pairs, local):
        first, last, shift = (0, SUBLANES - 1, 1) if d == 0 else (SUBLANES - 1, 0, SUBLANES - 1)
        hin = jnp.where(sub == first, jnp.broadcast_to(h0[d][s], (SUBLANES, LANES)), 0.0)
        for step in range(1, SUBLANES):
            k = first + step if d == 0 else first - step
            nxt = pltpu.roll(hfin + ptot * hin, shift, 0)
            hin = jnp.where(sub == k, nxt, hin)
        carries.append(hin)
        finals.append((hfin + ptot * hin)[last:last + 1, :])

    if write_states:
        def fix_body(j, _):
            for (d, s), hin in zip(pairs, carries):
                jj = row_of(d, j)
                src = pl.ds(pl.multiple_of(jj * SUBLANES, SUBLANES), SUBLANES)
                b_ref[d, s, pl.ds(jj, SUBLANES, stride=m), :] = sh_ref[d, s, src, :] + sp_ref[d, s, src, :] * hin
            return 0

        lax.fori_loop(0, m, fix_body, 0, unroll=2)

    return [[finals[d * n_slabs + s] for s in range(n_slabs)] for d in range(2)]


def _lru_kernel(ux_ref, uxc_ref, ug_ref, cw_ref, cb_ref, wg_ref, bg_ref, lam_ref, o_ref,
                xc_s, xcc_s, a_s, b_s, ac_s, bc_s, sh_s, sp_s):
    n = ux_ref.shape[1]
    n_ctx = uxc_ref.shape[1]
    n_slabs = ux_ref.shape[2] // LANES
    ls = _log_sigmoid(lam_ref[...])

    for ref, rows in ((a_s, n), (ac_s, n_ctx)):
        ref[:, :, rows:, :] = jnp.ones((2, n_slabs, ref.shape[2] - rows, LANES), F32)
    for ref, rows in ((b_s, n), (bc_s, n_ctx)):
        ref[:, :, rows:, :] = jnp.zeros((2, n_slabs, ref.shape[2] - rows, LANES), F32)

    left = LRU_CONV // 2
    xcc_s[...] = _depthwise_conv_rows(uxc_ref[0], cw_ref, cb_ref, left)
    xc_s[...] = _depthwise_conv_rows(ux_ref[0], cw_ref, cb_ref, left)

    _lru_gates(xcc_s, n_ctx, n_ctx, wg_ref, bg_ref, ls, ac_s, bc_s, n_slabs)
    _lru_gates(xc_s, n, LRU_ROW_CHUNK, wg_ref, bg_ref, ls, a_s, b_s, n_slabs)

    zero = jnp.zeros((1, LANES), F32)
    h0 = _lru_scan(ac_s, bc_s, None, None, n_ctx, [[zero] * n_slabs] * 2, n_slabs, False)
    _lru_scan(a_s, b_s, sh_s, sp_s, n, h0, n_slabs, True)

    for r0 in range(0, n, LRU_ROW_CHUNK):
        rows = slice(r0, r0 + LRU_ROW_CHUNK)
        for s in range(n_slabs):
            lanes = slice(s * LANES, (s + 1) * LANES)
            hsum = b_s[0, s, rows, :] + b_s[1, s, rows, :]
            gate = jax.nn.gelu(ug_ref[0, rows, lanes], approximate=True)
            o_ref[0, rows, lanes] = (hsum * gate).astype(o_ref.dtype)


def _rg_lru(ux, uxc, ug, conv_w, conv_b, wg, bg, lam):
    bsz, n, c = ux.shape
    n_ctx = uxc.shape[1]
    cb = LRU_CB
    n_slabs = cb // LANES
    lp = SUBLANES * _scan_pitch(n)
    cp = SUBLANES * _scan_pitch(n_ctx)
    seq = lambda rows: pl.BlockSpec((1, rows, cb), lambda b, j: (b, 0, j))
    par = lambda rows: pl.BlockSpec((rows, cb), lambda b, j: (0, j))
    return pl.pallas_call(
        _lru_kernel,
        grid=(bsz, c // cb),
        in_specs=[
            seq(n), seq(n_ctx), seq(n),
            par(conv_w.shape[0]), par(1),
            pl.BlockSpec((n_slabs, LANES, 4 * LANES), lambda b, j: (j, 0, 0)),
            par(4), par(2),
        ],
        out_specs=seq(n),
        out_shape=jax.ShapeDtypeStruct((bsz, n, c), BF16),
        scratch_shapes=[
            pltpu.VMEM((n, cb), F32),
            pltpu.VMEM((n_ctx, cb), F32),
            pltpu.VMEM((2, n_slabs, lp, LANES), F32),
            pltpu.VMEM((2, n_slabs, lp, LANES), F32),
            pltpu.VMEM((2, n_slabs, cp, LANES), F32),
            pltpu.VMEM((2, n_slabs, cp, LANES), F32),
            pltpu.VMEM((2, n_slabs, lp, LANES), F32),
            pltpu.VMEM((2, n_slabs, lp, LANES), F32),
        ],
        compiler_params=_cparams(("parallel", "parallel")),
        name="rg_lru",
    )(ux, uxc, ug, conv_w, conv_b.reshape(1, c), wg, bg, lam)


def _fourier_kernel(u_ref, csn_ref, csc_ref, wf_ref, bf_ref, o_ref):
    gw = u_ref.shape[2]
    p = jnp.dot(u_ref[0], csc_ref[...], preferred_element_type=F32)
    pp = jnp.concatenate([p[:, :gw], p[:, gw:]], axis=0).astype(BF16)
    f = jnp.dot(csn_ref[...], pp, preferred_element_type=F32)
    y = jnp.dot(f.astype(BF16), wf_ref[0], preferred_element_type=F32) + bf_ref[...]
    o_ref[0] = y.astype(o_ref.dtype)


def _dft_cos_sin(n, scale):
    idx = jnp.arange(n, dtype=jnp.int32)
    ang = ((idx[:, None] * idx[None, :]) % n).astype(F32) * (2.0 * math.pi / n)
    return jnp.cos(ang) * scale, jnp.sin(ang) * scale


def _fourier(uf, w_f, b_f):
    bsz, n, c = uf.shape
    groups, gw, _ = w_f.shape
    cn, sn = _dft_cos_sin(n, n ** -0.5)
    cc, sc = _dft_cos_sin(gw, gw ** -0.5)
    csn = jnp.concatenate([cn, -sn], axis=1).astype(BF16)
    csc = jnp.concatenate([cc, sc], axis=1).astype(BF16)
    return pl.pallas_call(
        _fourier_kernel,
        grid=(bsz, groups),
        in_specs=[
            pl.BlockSpec((1, n, gw), lambda b, g: (b, 0, g)),
            _const_spec(csn.shape),
            _const_spec(csc.shape),
            pl.BlockSpec((1, gw, gw), lambda b, g: (g, 0, 0)),
            pl.BlockSpec((1, gw), lambda b, g: (0, g)),
        ],
        out_specs=pl.BlockSpec((1, n, gw), lambda b, g: (b, 0, g)),
        out_shape=jax.ShapeDtypeStruct((bsz, n, c), BF16),
        compiler_params=_cparams(("parallel", "parallel")),
        name="fourier",
    )(uf, csn, csc, w_f.astype(BF16), b_f.reshape(1, c))


def _out_proj_kernel(yf_ref, yl_ref, x_ref, pe_ref, mod_ref, gpost_ref, gpre_ref, w_ref, x1_ref, h2_ref):
    k = yf_ref.shape[2]
    y = jnp.dot(yf_ref[0], w_ref[:k, :], preferred_element_type=F32)
    y = y + jnp.dot(yl_ref[0], w_ref[k:, :], preferred_element_type=F32)
    x1 = (x_ref[0] + pe_ref[...]) + mod_ref[0, 2:3, :] * _rms_norm(y, gpost_ref[...])
    x1_ref[0] = x1
    h2 = _rms_norm(x1, gpre_ref[...]) * (1.0 + mod_ref[0, 4:5, :]) + mod_ref[0, 3:4, :]
    h2_ref[0] = h2.astype(h2_ref.dtype)


def _out_proj(yf, yl, x, pe, mod3, g_post, g_pre, w_out):
    bsz, n, d = x.shape
    k = yf.shape[2]
    tm = IN_TM
    act = pl.BlockSpec((1, tm, k), lambda i, b: (b, i, 0))
    row = pl.BlockSpec((1, tm, d), lambda i, b: (b, i, 0))
    return pl.pallas_call(
        _out_proj_kernel,
        grid=(n // tm, bsz),
        in_specs=[
            act, act, row,
            pl.BlockSpec((tm, d), lambda i, b: (i, 0)),
            pl.BlockSpec((1, N_MOD, d), lambda i, b: (b, 0, 0)),
            _const_spec((1, d)), _const_spec((1, d)),
            _const_spec(w_out.shape),
        ],
        out_specs=[row, row],
        out_shape=[jax.ShapeDtypeStruct((bsz, n, d), F32), jax.ShapeDtypeStruct((bsz, n, d), BF16)],
        compiler_params=_cparams(("parallel", "parallel")),
        name="out_proj",
    )(yf, yl, x, pe, mod3, g_post.reshape(1, d), g_pre.reshape(1, d), w_out)


def _ffn_kernel(h_ref, hprev_ref, hnext_ref, wg_ref, wv_ref, cwg_ref, cwv_ref, cbg_ref, cbv_ref,
                wd_ref, x1_ref, mod_ref, gpost_ref, o_ref, lhs_s, acc_s):
    i = pl.program_id(1)
    c = pl.program_id(2)
    tm = h_ref.shape[1]
    halo = hprev_ref.shape[1]
    rows = tm + 2 * halo

    @pl.when(c == 0)
    def _():
        lhs_s[0:halo, :] = jnp.where(i > 0, hprev_ref[0], jnp.zeros_like(hprev_ref[0]))
        lhs_s[halo:halo + tm, :] = h_ref[0]
        lhs_s[halo + tm:, :] = jnp.where(i < pl.num_programs(1) - 1, hnext_ref[0], jnp.zeros_like(hnext_ref[0]))
        acc_s[...] = jnp.zeros_like(acc_s)

    def conv_branch(w_ref, cw_ref, cb_ref):
        up = jnp.dot(lhs_s[...], w_ref[...], preferred_element_type=F32)
        left = FFN_CONV // 2
        y = cb_ref[...] + jnp.zeros((tm, up.shape[1]), F32)
        for k in range(FFN_CONV):
            off = k - left
            shifted = up if off == 0 else pltpu.roll(up, (-off) % rows, 0)
            y = y + shifted[halo:halo + tm, :] * cw_ref[k:k + 1, :]
        return y

    g = conv_branch(wg_ref, cwg_ref, cbg_ref)
    v = conv_branch(wv_ref, cwv_ref, cbv_ref)
    act = (jax.nn.gelu(g, approximate=True) * v).astype(BF16)
    acc_s[...] += jnp.dot(act, wd_ref[...], preferred_element_type=F32)

    @pl.when(c == pl.num_programs(2) - 1)
    def _():
        o_ref[0] = x1_ref[0] + mod_ref[0, 5:6, :] * _rms_norm(acc_s[...], gpost_ref[...])


def _conv_ffn(h2, x1, mod3, g_post, w_up, conv_w, conv_b, w_down):
    bsz, n, d = h2.shape
    d_ff = w_down.shape[0]
    tm, tf, halo = FFN_TM, FFN_TF, FFN_HALO
    nt, nf = n // tm, d_ff // tf
    hb = tm // halo
    row = pl.BlockSpec((1, tm, d), lambda b, i, c: (b, i, 0))
    gcol = lambda r: pl.BlockSpec((r, tf), lambda b, i, c: (0, c))
    vcol = lambda r: pl.BlockSpec((r, tf), lambda b, i, c: (0, c + nf))
    conv_b2 = conv_b.reshape(1, 2 * d_ff)
    return pl.pallas_call(
        _ffn_kernel,
        grid=(bsz, nt, nf),
        in_specs=[
            row,
            pl.BlockSpec((1, halo, d), lambda b, i, c: (b, jnp.maximum(i * hb - 1, 0), 0)),
            pl.BlockSpec((1, halo, d), lambda b, i, c: (b, jnp.minimum((i + 1) * hb, n // halo - 1), 0)),
            gcol(d), vcol(d),
            gcol(FFN_CONV), vcol(FFN_CONV),
            gcol(1), vcol(1),
            pl.BlockSpec((tf, d), lambda b, i, c: (c, 0)),
            row,
            pl.BlockSpec((1, N_MOD, d), lambda b, i, c: (b, 0, 0)),
            _const_spec((1, d)),
        ],
        out_specs=row,
        out_shape=jax.ShapeDtypeStruct((bsz, n, d), F32),
        scratch_shapes=[
            pltpu.VMEM((tm + 2 * halo, d), BF16),
            pltpu.VMEM((tm, d), F32),
        ],
        compiler_params=_cparams(("parallel", "parallel", "arbitrary")),
        name="conv_ffn",
    )(h2, h2, h2, w_up, w_up, conv_w, conv_w, conv_b2, conv_b2, w_down, x1, mod3, g_post.reshape(1, d))


def _grid_pos_embed(n_tokens, d_model):
    rows = n_tokens // GRID_W
    row = jnp.repeat(jnp.arange(rows, dtype=F32), GRID_W)
    col = jnp.tile(jnp.arange(GRID_W, dtype=F32), rows)
    quarter = d_model // 4
    freqs = POS_BASE ** (-jnp.arange(quarter, dtype=F32) / quarter)

    def enc(p):
        ang = p[:, None] * freqs[None, :]
        return jnp.concatenate([jnp.sin(ang), jnp.cos(ang)], axis=-1)

    return jnp.concatenate([enc(row), enc(col)], axis=-1)


def kernel(x, c, ctx, c_ctx, w_ada, b_ada, g_mix_pre, g_mix_post, g_ffn_pre, g_ffn_post,
           w_in, conv_lru_w, conv_lru_b, w_rec_gate, b_rec_gate, w_in_gate, b_in_gate,
           lru_lambda, w_fourier, b_fourier, w_out, w_up, conv_ffn_w, conv_ffn_b, w_down):
    bsz, n, d = x.shape
    d_fourier = w_fourier.shape[1] * w_fourier.shape[2]
    d_lru = conv_lru_w.shape[2]
    assert w_ada.shape[0] == 1, "single layer: the context stream is only read, never updated"
    assert bsz + 1 <= MOD_ROWS and d_lru // N_LRU_HEADS == LANES
    l = 0
    pe = _grid_pos_embed(n, d)

    cond = jnp.zeros((MOD_ROWS, d), F32).at[:bsz].set(c).at[bsz].set(c_ctx)
    mod3 = _modulation(cond, w_ada[l], b_ada[l]).reshape(MOD_ROWS, N_MOD, d)

    w_in_b = w_in[l].astype(BF16)
    x_cols = (d_fourier, d_fourier + d_lru)
    (uxc,) = _in_proj(ctx, None, mod3, lambda b: bsz, g_mix_pre[l], w_in_b[:, x_cols[0]:x_cols[1]],
                      ((0, d_lru),), (F32,), ctx.shape[1], "in_proj_ctx")
    uf, ux, ug = _in_proj(x, pe, mod3, lambda b: b, g_mix_pre[l], w_in_b,
                          ((0, d_fourier), x_cols, (x_cols[1], x_cols[1] + d_lru)),
                          (BF16, F32, F32), IN_TM, "in_proj")

    wg = jnp.concatenate([w_rec_gate[l, 0], w_in_gate[l, 0], w_rec_gate[l, 1], w_in_gate[l, 1]],
                         axis=-1).astype(BF16)
    bg = jnp.stack([b_rec_gate[l, 0], b_in_gate[l, 0], b_rec_gate[l, 1], b_in_gate[l, 1]])
    y_lru = _rg_lru(ux, uxc, ug, conv_lru_w[l], conv_lru_b[l], wg, bg, lru_lambda[l])
    y_fourier = _fourier(uf, w_fourier[l], b_fourier[l])

    x1, h2 = _out_proj(y_fourier, y_lru, x, pe, mod3, g_mix_post[l], g_ffn_pre[l], w_out[l].astype(BF16))
    return _conv_ffn(h2, x1, mod3, g_ffn_post[l], w_up[l].astype(BF16), conv_ffn_w[l], conv_ffn_b[l],
                     w_down[l].astype(BF16))
```

```python
import functools
import math

import jax
import jax.numpy as jnp
from jax import lax
from jax.experimental import pallas as pl
from jax.experimental.pallas import tpu as pltpu

F32 = jnp.float32
BF16 = jnp.bfloat16

EPS = 1e-6
POS_BASE = 10000.0
GRID_W = 64
N_MOD = 6
N_FOURIER_GROUPS = 4
N_LRU_HEADS = 8
LRU_CONV = 4
LRU_C = 8.0
FFN_CONV = 3

LANES = 128
SUBLANES = 8
VMEM_LIMIT_BYTES = 60000 * 1024

MOD_ROWS = 16
IN_TM = 512
LRU_CB = 256
LRU_ROW_CHUNK = 512
SCAN_UNROLL = 4
FFN_TM = 512
FFN_TF = 512
FFN_HALO = 16


def _cparams(sem):
    return pltpu.CompilerParams(dimension_semantics=sem, vmem_limit_bytes=VMEM_LIMIT_BYTES)


def _const_spec(shape):
    nd = len(shape)
    return pl.BlockSpec(shape, lambda *_: (0,) * nd, pipeline_mode=pl.Buffered(1))


def _mod_kernel(c_ref, w_ref, b_ref, o_ref):
    cv = c_ref[...]
    s = (cv * jax.nn.sigmoid(cv)).astype(BF16)
    o_ref[...] = jnp.dot(s, w_ref[...].astype(BF16), preferred_element_type=F32) + b_ref[...]


def _modulation(cond, w_ada, b_ada):
    d = cond.shape[1]
    n = w_ada.shape[1]
    tn = 1024
    return pl.pallas_call(
        _mod_kernel,
        grid=(n // tn,),
        in_specs=[
            pl.BlockSpec((MOD_ROWS, d), lambda j: (0, 0)),
            pl.BlockSpec((d, tn), lambda j: (0, j)),
            pl.BlockSpec((1, tn), lambda j: (0, j)),
        ],
        out_specs=pl.BlockSpec((MOD_ROWS, tn), lambda j: (0, j)),
        out_shape=jax.ShapeDtypeStruct((MOD_ROWS, n), F32),
        compiler_params=_cparams(("parallel",)),
        name="modulation",
    )(cond, w_ada, b_ada.reshape(1, n))


def _rms_norm(v, g):
    ms = jnp.mean(v * v, axis=-1, keepdims=True)
    return v * lax.rsqrt(ms + EPS) * g


def _tile_pos_embed(prow_ref, pcol_ref):
    rows = jnp.concatenate([jnp.broadcast_to(prow_ref[r:r + 1, :], pcol_ref.shape)
                            for r in range(prow_ref.shape[0])], axis=0)
    cols = jnp.concatenate([pcol_ref[...]] * prow_ref.shape[0], axis=0)
    return jnp.concatenate([rows, cols], axis=1)


def _in_proj_kernel(*refs, has_pe, col_splits):
    if has_pe:
        x_ref, prow_ref, pcol_ref, mod_ref, g_ref, w_ref = refs[:6]
        out_refs = refs[6:]
        xp = x_ref[0] + _tile_pos_embed(prow_ref, pcol_ref)
    else:
        x_ref, mod_ref, g_ref, w_ref = refs[:4]
        out_refs = refs[4:]
        xp = x_ref[0]
    h = _rms_norm(xp, g_ref[...]) * (1.0 + mod_ref[0, 1:2, :]) + mod_ref[0, 0:1, :]
    hb = h.astype(BF16)
    for o_ref, (c0, c1) in zip(out_refs, col_splits):
        o_ref[0] = jnp.dot(hb, w_ref[:, c0:c1], preferred_element_type=F32).astype(o_ref.dtype)


def _pos_embed_specs(pe, tm):
    prow, pcol = pe
    assert tm % GRID_W == 0
    return [pl.BlockSpec((tm // GRID_W, prow.shape[1]), lambda i, b: (i, 0)), _const_spec(pcol.shape)]


def _in_proj(x, pe, mod3, mod_row, g, w, col_splits, out_dtypes, tm, name):
    bsz, n, d = x.shape
    has_pe = pe is not None
    in_specs = [pl.BlockSpec((1, tm, d), lambda i, b: (b, i, 0))]
    args = [x]
    if has_pe:
        in_specs += _pos_embed_specs(pe, tm)
        args += list(pe)
    in_specs += [
        pl.BlockSpec((1, N_MOD, d), lambda i, b: (mod_row(b), 0, 0)),
        _const_spec((1, d)),
        _const_spec(w.shape),
    ]
    args += [mod3, g.reshape(1, d), w]
    out_specs = [pl.BlockSpec((1, tm, c1 - c0), lambda i, b: (b, i, 0)) for c0, c1 in col_splits]
    out_shape = [jax.ShapeDtypeStruct((bsz, n, c1 - c0), dt) for (c0, c1), dt in zip(col_splits, out_dtypes)]
    return pl.pallas_call(
        functools.partial(_in_proj_kernel, has_pe=has_pe, col_splits=col_splits),
        grid=(n // tm, bsz),
        in_specs=in_specs,
        out_specs=out_specs,
        out_shape=out_shape,
        compiler_params=_cparams(("parallel", "parallel")),
        name=name,
    )(*args)


def _scan_pitch(n):
    m = -(-n // SUBLANES)
    while m % 8 != 4:
        m += 1
    return m


def _depthwise_conv_rows(x, w_ref, b_ref, left):
    n = x.shape[0]
    k_w = w_ref.shape[0]
    row = lax.broadcasted_iota(jnp.int32, x.shape, 0)
    y = b_ref[...] + jnp.zeros_like(x)
    for k in range(k_w):
        off = k - left
        if off == 0:
            xs = x
        elif off < 0:
            xs = jnp.where(row >= -off, pltpu.roll(x, -off, 0), 0.0)
        else:
            xs = jnp.where(row < n - off, pltpu.roll(x, n - off, 0), 0.0)
        y = y + xs * w_ref[k:k + 1, :]
    return y


def _log_sigmoid(z):
    return jnp.minimum(z, 0.0) - jnp.log1p(jnp.exp(-jnp.abs(z)))


def _lru_gates(xc_ref, n, row_chunk, wg_ref, bg_ref, quarter_c_ls, a_ref, b_ref, n_slabs):
    for r0 in range(0, n, row_chunk):
        for s in range(n_slabs):
            lanes = slice(s * LANES, (s + 1) * LANES)
            xh = xc_ref[r0:r0 + row_chunk, lanes]
            g = jnp.dot(xh.astype(BF16), wg_ref[s], preferred_element_type=F32)
            for d in range(2):
                tr = jnp.tanh(g[:, (2 * d) * LANES:(2 * d + 1) * LANES] + bg_ref[2 * d:2 * d + 1, lanes])
                ti = jnp.tanh(g[:, (2 * d + 1) * LANES:(2 * d + 2) * LANES] + bg_ref[2 * d + 1:2 * d + 2, lanes])
                hl = quarter_c_ls[d:d + 1, lanes]
                t = jnp.tanh(hl + hl * tr)
                q = pl.reciprocal(1.0 - t, full_range=False)
                nt = -t
                root = jnp.where(nt > 0.0, nt * lax.rsqrt(nt), 0.0)
                a_ref[d, s, r0:r0 + row_chunk, :] = (1.0 + t) * q
                b_ref[d, s, r0:r0 + row_chunk, :] = (root * q) * ((1.0 + ti) * xh)


def _lru_scan(a_ref, b_ref, sh_ref, sp_ref, n, h0, n_slabs, write_states):
    m = _scan_pitch(n)
    pairs = [(d, s) for d in range(2) for s in range(n_slabs)]
    sub = lax.broadcasted_iota(jnp.int32, (SUBLANES, LANES), 0)

    def row_of(d, j):
        return j if d == 0 else m - 1 - j

    def local_body(j, carry):
        out = []
        for (d, s), (h, p) in zip(pairs, carry):
            jj = row_of(d, j)
            a = a_ref[d, s, pl.ds(jj, SUBLANES, stride=m), :]
            b = b_ref[d, s, pl.ds(jj, SUBLANES, stride=m), :]
            h = a * h + b
            p = a * p
            if write_states:
                dst = pl.ds(pl.multiple_of(jj * SUBLANES, SUBLANES), SUBLANES)
                sh_ref[d, s, dst, :] = h
                sp_ref[d, s, dst, :] = p
            out.append((h, p))
        return tuple(out)

    init = tuple((jnp.zeros((SUBLANES, LANES), F32), jnp.ones((SUBLANES, LANES), F32)) for _ in pairs)
    local = lax.fori_loop(0, m, local_body, init, unroll=SCAN_UNROLL)

    carries, finals = [], []
    for (d, s), (hfin, ptot) in zip(pairs, local):
        first, last, shift = (0, SUBLANES - 1, 1) if d == 0 else (SUBLANES - 1, 0, SUBLANES - 1)
        hin = jnp.where(sub == first, jnp.broadcast_to(h0[d][s], (SUBLANES, LANES)), 0.0)
        for step in range(1, SUBLANES):
            k = first + step if d == 0 else first - step
            nxt = pltpu.roll(hfin + ptot * hin, shift, 0)
            hin = jnp.where(sub == k, nxt, hin)
        carries.append(hin)
        finals.append((hfin + ptot * hin)[last:last + 1, :])

    if write_states:
        def fix_body(j, _):
            for (d, s), hin in zip(pairs, carries):
                jj = row_of(d, j)
                src = pl.ds(pl.multiple_of(jj * SUBLANES, SUBLANES), SUBLANES)
                b_ref[d, s, pl.ds(jj, SUBLANES, stride=m), :] = sh_ref[d, s, src, :] + sp_ref[d, s, src, :] * hin
            return 0

        lax.fori_loop(0, m, fix_body, 0, unroll=SCAN_UNROLL)

    return [[finals[d * n_slabs + s] for s in range(n_slabs)] for d in range(2)]


def _lru_kernel(ux_ref, uxc_ref, ug_ref, cw_ref, cb_ref, wg_ref, bg_ref, lam_ref, o_ref,
                xc_s, xcc_s, a_s, b_s, ac_s, bc_s, sh_s, sp_s):
    n = ux_ref.shape[1]
    n_ctx = uxc_ref.shape[1]
    n_slabs = ux_ref.shape[2] // LANES
    ls = (0.25 * LRU_C) * _log_sigmoid(lam_ref[...])

    for ref, rows in ((a_s, n), (ac_s, n_ctx)):
        ref[:, :, rows:, :] = jnp.ones((2, n_slabs, ref.shape[2] - rows, LANES), F32)
    for ref, rows in ((b_s, n), (bc_s, n_ctx)):
        ref[:, :, rows:, :] = jnp.zeros((2, n_slabs, ref.shape[2] - rows, LANES), F32)

    left = LRU_CONV // 2
    xcc_s[...] = _depthwise_conv_rows(uxc_ref[0], cw_ref, cb_ref, left)
    xc_s[...] = _depthwise_conv_rows(ux_ref[0], cw_ref, cb_ref, left)

    _lru_gates(xcc_s, n_ctx, n_ctx, wg_ref, bg_ref, ls, ac_s, bc_s, n_slabs)
    _lru_gates(xc_s, n, LRU_ROW_CHUNK, wg_ref, bg_ref, ls, a_s, b_s, n_slabs)

    zero = jnp.zeros((1, LANES), F32)
    h0 = _lru_scan(ac_s, bc_s, None, None, n_ctx, [[zero] * n_slabs] * 2, n_slabs, False)
    _lru_scan(a_s, b_s, sh_s, sp_s, n, h0, n_slabs, True)

    for r0 in range(0, n, LRU_ROW_CHUNK):
        rows = slice(r0, r0 + LRU_ROW_CHUNK)
        for s in range(n_slabs):
            lanes = slice(s * LANES, (s + 1) * LANES)
            hsum = b_s[0, s, rows, :] + b_s[1, s, rows, :]
            gate = jax.nn.gelu(ug_ref[0, rows, lanes], approximate=True)
            o_ref[0, rows, lanes] = (hsum * gate).astype(o_ref.dtype)


def _rg_lru(ux, uxc, ug, conv_w, conv_b, wg, bg, lam):
    bsz, n, c = ux.shape
    n_ctx = uxc.shape[1]
    cb = LRU_CB
    n_slabs = cb // LANES
    lp = SUBLANES * _scan_pitch(n)
    cp = SUBLANES * _scan_pitch(n_ctx)
    seq = lambda rows: pl.BlockSpec((1, rows, cb), lambda b, j: (b, 0, j))
    par = lambda rows: pl.BlockSpec((rows, cb), lambda b, j: (0, j))
    return pl.pallas_call(
        _lru_kernel,
        grid=(bsz, c // cb),
        in_specs=[
            seq(n), seq(n_ctx), seq(n),
            par(conv_w.shape[0]), par(1),
            pl.BlockSpec((n_slabs, LANES, 4 * LANES), lambda b, j: (j, 0, 0)),
            par(4), par(2),
        ],
        out_specs=seq(n),
        out_shape=jax.ShapeDtypeStruct((bsz, n, c), BF16),
        scratch_shapes=[
            pltpu.VMEM((n, cb), F32),
            pltpu.VMEM((n_ctx, cb), F32),
            pltpu.VMEM((2, n_slabs, lp, LANES), F32),
            pltpu.VMEM((2, n_slabs, lp, LANES), F32),
            pltpu.VMEM((2, n_slabs, cp, LANES), F32),
            pltpu.VMEM((2, n_slabs, cp, LANES), F32),
            pltpu.VMEM((2, n_slabs, lp, LANES), F32),
            pltpu.VMEM((2, n_slabs, lp, LANES), F32),
        ],
        compiler_params=_cparams(("parallel", "parallel")),
        name="rg_lru",
    )(ux, uxc, ug, conv_w, conv_b.reshape(1, c), wg, bg, lam)


FFT_RADIX = 8
_FFT_BLOCK_OF = {0: (0, 0.0), 4: (4, 0.0), 2: (2, -1.0), 6: (2, 1.0), 1: (1, -1.0), 7: (1, 1.0), 3: (3, 1.0), 5: (3, -1.0)}


def _fourier_kernel(u_ref, m_ref, ccs_ref, wf_ref, bf_ref, o_ref, y_s):
    n = u_ref.shape[1]
    r = n // FFT_RADIX
    groups, gw, _ = wf_ref.shape
    x = [u_ref[0, p * r:(p + 1) * r, :].astype(F32) for p in range(FFT_RADIX)]
    e0, e1, e2, e3 = x[0] + x[4], x[0] - x[4], x[2] + x[6], x[2] - x[6]
    o0, o1, o2, o3 = x[1] + x[5], x[1] - x[5], x[3] + x[7], x[3] - x[7]
    ee, oo = e0 + e2, o0 + o2
    c = math.sqrt(0.5)
    p, m = c * (o1 - o3), c * (o1 + o3)
    t_re = {0: ee + oo, 4: ee - oo, 2: e0 - e2, 1: e1 + p, 3: e1 - p}
    t_im = {2: o0 - o2, 1: m + e3, 3: e3 - m}

    for k1 in range(FFT_RADIX):
        blk = _FFT_BLOCK_OF[k1][0]
        if blk in t_im:
            rhs = jnp.concatenate([t_re[blk], t_im[blk]], axis=0).astype(BF16)
            z = jnp.dot(m_ref[k1], rhs, preferred_element_type=F32)
        else:
            z = jnp.dot(m_ref[k1, :, :r], t_re[blk].astype(BF16), preferred_element_type=F32)
        for g in range(groups):
            cols = slice(g * gw, (g + 1) * gw)
            f = jnp.dot(z[:r, cols].astype(BF16), ccs_ref[0], preferred_element_type=F32)
            f = f + jnp.dot(z[r:, cols].astype(BF16), ccs_ref[1], preferred_element_type=F32)
            y = jnp.dot(f.astype(BF16), wf_ref[g], preferred_element_type=F32) + bf_ref[:, cols]
            for h in range(gw // LANES):
                slab = g * (gw // LANES) + h
                y_s[slab, pl.ds(k1, r, stride=FFT_RADIX), :] = y[:, h * LANES:(h + 1) * LANES]

    for slab in range(y_s.shape[0]):
        o_ref[0, :, slab * LANES:(slab + 1) * LANES] = y_s[slab].astype(o_ref.dtype)


def _fourier_tables(n, gw):
    r = n // FFT_RADIX
    k = jnp.arange(FFT_RADIX, dtype=jnp.int32)[:, None, None] + FFT_RADIX * jnp.arange(r, dtype=jnp.int32)[None, :, None]
    p2 = jnp.arange(r, dtype=jnp.int32)[None, None, :]
    ang = ((k * p2) % n).astype(F32) * (2.0 * math.pi / n)
    cn, sn = jnp.cos(ang) * n ** -0.5, jnp.sin(ang) * n ** -0.5
    sign = jnp.array([_FFT_BLOCK_OF[k1][1] for k1 in range(FFT_RADIX)], F32)[:, None, None]
    m = jnp.concatenate([jnp.concatenate([cn, sign * sn], axis=2),
                         jnp.concatenate([-sn, sign * cn], axis=2)], axis=1)
    idx = jnp.arange(gw, dtype=jnp.int32)
    ang_c = ((idx[:, None] * idx[None, :]) % gw).astype(F32) * (2.0 * math.pi / gw)
    ccs = jnp.stack([jnp.cos(ang_c), jnp.sin(ang_c)]) * gw ** -0.5
    return m.astype(BF16), ccs.astype(BF16)


def _fourier(uf, w_f, b_f):
    bsz, n, c = uf.shape
    groups, gw, _ = w_f.shape
    assert n % FFT_RADIX == 0 and gw % LANES == 0
    m, ccs = _fourier_tables(n, gw)
    return pl.pallas_call(
        _fourier_kernel,
        grid=(bsz,),
        in_specs=[
            pl.BlockSpec((1, n, c), lambda b: (b, 0, 0)),
            _const_spec(m.shape),
            _const_spec(ccs.shape),
            _const_spec(w_f.shape),
            _const_spec((1, c)),
        ],
        out_specs=pl.BlockSpec((1, n, c), lambda b: (b, 0, 0)),
        out_shape=jax.ShapeDtypeStruct((bsz, n, c), BF16),
        scratch_shapes=[pltpu.VMEM((c // LANES, n, LANES), F32)],
        compiler_params=_cparams(("parallel",)),
        name="fourier",
    )(uf, m, ccs, w_f.astype(BF16), b_f.reshape(1, c))


def _out_proj_kernel(yf_ref, yl_ref, x_ref, prow_ref, pcol_ref, mod_ref, gpost_ref, gpre_ref, w_ref,
                     x1_ref, h2_ref):
    k = yf_ref.shape[2]
    y = jnp.dot(yf_ref[0], w_ref[:k, :], preferred_element_type=F32)
    y = y + jnp.dot(yl_ref[0], w_ref[k:, :], preferred_element_type=F32)
    xp = x_ref[0] + _tile_pos_embed(prow_ref, pcol_ref)
    x1 = xp + mod_ref[0, 2:3, :] * _rms_norm(y, gpost_ref[...])
    x1_ref[0] = x1
    h2 = _rms_norm(x1, gpre_ref[...]) * (1.0 + mod_ref[0, 4:5, :]) + mod_ref[0, 3:4, :]
    h2_ref[0] = h2.astype(h2_ref.dtype)


def _out_proj(yf, yl, x, pe, mod3, g_post, g_pre, w_out):
    bsz, n, d = x.shape
    k = yf.shape[2]
    tm = IN_TM
    act = pl.BlockSpec((1, tm, k), lambda i, b: (b, i, 0))
    row = pl.BlockSpec((1, tm, d), lambda i, b: (b, i, 0))
    return pl.pallas_call(
        _out_proj_kernel,
        grid=(n // tm, bsz),
        in_specs=[
            act, act, row,
            *_pos_embed_specs(pe, tm),
            pl.BlockSpec((1, N_MOD, d), lambda i, b: (b, 0, 0)),
            _const_spec((1, d)), _const_spec((1, d)),
            _const_spec(w_out.shape),
        ],
        out_specs=[row, row],
        out_shape=[jax.ShapeDtypeStruct((bsz, n, d), F32), jax.ShapeDtypeStruct((bsz, n, d), BF16)],
        compiler_params=_cparams(("parallel", "parallel")),
        name="out_proj",
    )(yf, yl, x, *pe, mod3, g_post.reshape(1, d), g_pre.reshape(1, d), w_out)


def _ffn_kernel(h_ref, hprev_ref, hnext_ref, wg_ref, wv_ref, cwg_ref, cwv_ref, cbg_ref, cbv_ref,
                wd_ref, x1_ref, mod_ref, gpost_ref, o_ref, lhs_s, acc_s):
    i = pl.program_id(1)
    c = pl.program_id(2)
    tm = h_ref.shape[1]
    halo = hprev_ref.shape[1]
    rows = tm + 2 * halo

    @pl.when(c == 0)
    def _():
        lhs_s[0:halo, :] = jnp.where(i > 0, hprev_ref[0], jnp.zeros_like(hprev_ref[0]))
        lhs_s[halo:halo + tm, :] = h_ref[0]
        lhs_s[halo + tm:, :] = jnp.where(i < pl.num_programs(1) - 1, hnext_ref[0], jnp.zeros_like(hnext_ref[0]))
        acc_s[...] = jnp.zeros_like(acc_s)

    def conv_branch(w_ref, cw_ref, cb_ref):
        up = jnp.dot(lhs_s[...], w_ref[...], preferred_element_type=F32)
        left = FFN_CONV // 2
        y = cb_ref[...] + jnp.zeros((tm, up.shape[1]), F32)
        for k in range(FFN_CONV):
            off = k - left
            shifted = up if off == 0 else pltpu.roll(up, (-off) % rows, 0)
            y = y + shifted[halo:halo + tm, :] * cw_ref[k:k + 1, :]
        return y

    g = conv_branch(wg_ref, cwg_ref, cbg_ref)
    v = conv_branch(wv_ref, cwv_ref, cbv_ref)
    act = (jax.nn.gelu(g, approximate=True) * v).astype(BF16)
    acc_s[...] += jnp.dot(act, wd_ref[...], preferred_element_type=F32)

    @pl.when(c == pl.num_programs(2) - 1)
    def _():
        o_ref[0] = x1_ref[0] + mod_ref[0, 5:6, :] * _rms_norm(acc_s[...], gpost_ref[...])


def _conv_ffn(h2, x1, mod3, g_post, w_up, conv_w, conv_b, w_down):
    bsz, n, d = h2.shape
    d_ff = w_down.shape[0]
    tm, tf, halo = FFN_TM, FFN_TF, FFN_HALO
    nt, nf = n // tm, d_ff // tf
    hb = tm // halo
    row = pl.BlockSpec((1, tm, d), lambda b, i, c: (b, i, 0))
    gcol = lambda r: pl.BlockSpec((r, tf), lambda b, i, c: (0, c))
    vcol = lambda r: pl.BlockSpec((r, tf), lambda b, i, c: (0, c + nf))
    conv_b2 = conv_b.reshape(1, 2 * d_ff)
    return pl.pallas_call(
        _ffn_kernel,
        grid=(bsz, nt, nf),
        in_specs=[
            row,
            pl.BlockSpec((1, halo, d), lambda b, i, c: (b, jnp.maximum(i * hb - 1, 0), 0)),
            pl.BlockSpec((1, halo, d), lambda b, i, c: (b, jnp.minimum((i + 1) * hb, n // halo - 1), 0)),
            gcol(d), vcol(d),
            gcol(FFN_CONV), vcol(FFN_CONV),
            gcol(1), vcol(1),
            pl.BlockSpec((tf, d), lambda b, i, c: (c, 0)),
            row,
            pl.BlockSpec((1, N_MOD, d), lambda b, i, c: (b, 0, 0)),
            _const_spec((1, d)),
        ],
        out_specs=row,
        out_shape=jax.ShapeDtypeStruct((bsz, n, d), F32),
        scratch_shapes=[
            pltpu.VMEM((tm + 2 * halo, d), BF16),
            pltpu.VMEM((tm, d), F32),
        ],
        compiler_params=_cparams(("parallel", "parallel", "arbitrary")),
        name="conv_ffn",
    )(h2, h2, h2, w_up, w_up, conv_w, conv_w, conv_b2, conv_b2, w_down, x1, mod3, g_post.reshape(1, d))


def _grid_pos_tables(n_tokens, d_model):
    quarter = d_model // 4
    freqs = POS_BASE ** (-jnp.arange(quarter, dtype=F32) / quarter)

    def enc(count):
        ang = jnp.arange(count, dtype=F32)[:, None] * freqs[None, :]
        return jnp.concatenate([jnp.sin(ang), jnp.cos(ang)], axis=-1)

    return enc(n_tokens // GRID_W), enc(GRID_W)


def kernel(x, c, ctx, c_ctx, w_ada, b_ada, g_mix_pre, g_mix_post, g_ffn_pre, g_ffn_post,
           w_in, conv_lru_w, conv_lru_b, w_rec_gate, b_rec_gate, w_in_gate, b_in_gate,
           lru_lambda, w_fourier, b_fourier, w_out, w_up, conv_ffn_w, conv_ffn_b, w_down):
    bsz, n, d = x.shape
    d_fourier = w_fourier.shape[1] * w_fourier.shape[2]
    d_lru = conv_lru_w.shape[2]
    assert w_ada.shape[0] == 1, "single layer: the context stream is only read, never updated"
    assert bsz + 1 <= MOD_ROWS and d_lru // N_LRU_HEADS == LANES
    l = 0
    pe = _grid_pos_tables(n, d)

    cond = jnp.zeros((MOD_ROWS, d), F32).at[:bsz].set(c).at[bsz].set(c_ctx)
    mod3 = _modulation(cond, w_ada[l], b_ada[l]).reshape(MOD_ROWS, N_MOD, d)

    w_in_b = w_in[l].astype(BF16)
    x_cols = (d_fourier, d_fourier + d_lru)
    (uxc,) = _in_proj(ctx, None, mod3, lambda b: bsz, g_mix_pre[l], w_in_b[:, x_cols[0]:x_cols[1]],
                      ((0, d_lru),), (F32,), ctx.shape[1], "in_proj_ctx")
    uf, ux, ug = _in_proj(x, pe, mod3, lambda b: b, g_mix_pre[l], w_in_b,
                          ((0, d_fourier), x_cols, (x_cols[1], x_cols[1] + d_lru)),
                          (BF16, F32, F32), IN_TM, "in_proj")

    wg = (0.5 * jnp.concatenate([w_rec_gate[l, 0], w_in_gate[l, 0], w_rec_gate[l, 1], w_in_gate[l, 1]],
                                axis=-1)).astype(BF16)
    bg = 0.5 * jnp.stack([b_rec_gate[l, 0], b_in_gate[l, 0], b_rec_gate[l, 1], b_in_gate[l, 1]])
    y_lru = _rg_lru(ux, uxc, ug, conv_lru_w[l], conv_lru_b[l], wg, bg, lru_lambda[l])
    y_fourier = _fourier(uf, w_fourier[l], b_fourier[l])

    x1, h2 = _out_proj(y_fourier, y_lru, x, pe, mod3, g_mix_post[l], g_ffn_pre[l], w_out[l].astype(BF16))
    return _conv_ffn(h2, x1, mod3, g_ffn_post[l], w_up[l].astype(BF16), conv_ffn_w[l], conv_ffn_b[l],
                     w_down[l].astype(BF16))
```

```python
import functools
import math

import jax
import jax.numpy as jnp
from jax import lax
from jax.experimental import pallas as pl
from jax.experimental.pallas import tpu as pltpu

F32 = jnp.float32
BF16 = jnp.bfloat16

EPS = 1e-6
POS_BASE = 10000.0
GRID_W = 64
N_MOD = 6
N_FOURIER_GROUPS = 4
N_LRU_HEADS = 8
LRU_CONV = 4
LRU_C = 8.0
FFN_CONV = 3

LANES = 128
SUBLANES = 8
VMEM_BYTES = 64 * 1024 * 1024
VMEM_LIMIT_BYTES = 60000 * 1024
FFN_VMEM_LIMIT_BYTES = VMEM_BYTES - 2 * 1024 * 1024

MOD_ROWS = 16
IN_TM = 512
LRU_CB = 256
LRU_ROW_CHUNK = 512
SCAN_UNROLL = 4
FFN_TM = 1024
FFN_TF = 512
FFN_HALO = 16


def _cparams(sem, vmem_limit_bytes=VMEM_LIMIT_BYTES):
    return pltpu.CompilerParams(dimension_semantics=sem, vmem_limit_bytes=vmem_limit_bytes)


def _const_spec(shape):
    nd = len(shape)
    return pl.BlockSpec(shape, lambda *_: (0,) * nd, pipeline_mode=pl.Buffered(1))


def _mod_kernel(c_ref, w_ref, b_ref, o_ref):
    cv = c_ref[...]
    s = (cv * jax.nn.sigmoid(cv)).astype(BF16)
    o_ref[...] = jnp.dot(s, w_ref[...].astype(BF16), preferred_element_type=F32) + b_ref[...]


def _modulation(cond, w_ada, b_ada):
    d = cond.shape[1]
    n = w_ada.shape[1]
    tn = 1024
    return pl.pallas_call(
        _mod_kernel,
        grid=(n // tn,),
        in_specs=[
            pl.BlockSpec((MOD_ROWS, d), lambda j: (0, 0)),
            pl.BlockSpec((d, tn), lambda j: (0, j)),
            pl.BlockSpec((1, tn), lambda j: (0, j)),
        ],
        out_specs=pl.BlockSpec((MOD_ROWS, tn), lambda j: (0, j)),
        out_shape=jax.ShapeDtypeStruct((MOD_ROWS, n), F32),
        compiler_params=_cparams(("parallel",)),
        name="modulation",
    )(cond, w_ada, b_ada.reshape(1, n))


def _rms_norm(v, g):
    ms = jnp.mean(v * v, axis=-1, keepdims=True)
    return v * lax.rsqrt(ms + EPS) * g


def _tile_pos_embed(prow_ref, pcol_ref):
    rows = jnp.concatenate([jnp.broadcast_to(prow_ref[r:r + 1, :], pcol_ref.shape)
                            for r in range(prow_ref.shape[0])], axis=0)
    cols = jnp.concatenate([pcol_ref[...]] * prow_ref.shape[0], axis=0)
    return jnp.concatenate([rows, cols], axis=1)


def _in_proj_kernel(*refs, has_pe, col_splits):
    if has_pe:
        x_ref, prow_ref, pcol_ref, mod_ref, g_ref, w_ref = refs[:6]
        out_refs = refs[6:]
        xp = x_ref[0] + _tile_pos_embed(prow_ref, pcol_ref)
    else:
        x_ref, mod_ref, g_ref, w_ref = refs[:4]
        out_refs = refs[4:]
        xp = x_ref[0]
    h = _rms_norm(xp, g_ref[...]) * (1.0 + mod_ref[0, 1:2, :]) + mod_ref[0, 0:1, :]
    hb = h.astype(BF16)
    for o_ref, (c0, c1) in zip(out_refs, col_splits):
        o_ref[0] = jnp.dot(hb, w_ref[:, c0:c1], preferred_element_type=F32).astype(o_ref.dtype)


def _pos_embed_specs(pe, tm):
    prow, pcol = pe
    assert tm % GRID_W == 0
    return [pl.BlockSpec((tm // GRID_W, prow.shape[1]), lambda i, b: (i, 0)), _const_spec(pcol.shape)]


def _in_proj(x, pe, mod3, mod_row, g, w, col_splits, out_dtypes, tm, name):
    bsz, n, d = x.shape
    has_pe = pe is not None
    in_specs = [pl.BlockSpec((1, tm, d), lambda i, b: (b, i, 0))]
    args = [x]
    if has_pe:
        in_specs += _pos_embed_specs(pe, tm)
        args += list(pe)
    in_specs += [
        pl.BlockSpec((1, N_MOD, d), lambda i, b: (mod_row(b), 0, 0)),
        _const_spec((1, d)),
        _const_spec(w.shape),
    ]
    args += [mod3, g.reshape(1, d), w]
    out_specs = [pl.BlockSpec((1, tm, c1 - c0), lambda i, b: (b, i, 0)) for c0, c1 in col_splits]
    out_shape = [jax.ShapeDtypeStruct((bsz, n, c1 - c0), dt) for (c0, c1), dt in zip(col_splits, out_dtypes)]
    return pl.pallas_call(
        functools.partial(_in_proj_kernel, has_pe=has_pe, col_splits=col_splits),
        grid=(n // tm, bsz),
        in_specs=in_specs,
        out_specs=out_specs,
        out_shape=out_shape,
        compiler_params=_cparams(("parallel", "parallel")),
        name=name,
    )(*args)


def _scan_pitch(n):
    m = -(-n // SUBLANES)
    while m % 8 != 4:
        m += 1
    return m


def _depthwise_conv_rows(x, w_ref, b_ref, left):
    n = x.shape[0]
    k_w = w_ref.shape[0]
    row = lax.broadcasted_iota(jnp.int32, x.shape, 0)
    y = b_ref[...] + jnp.zeros_like(x)
    for k in range(k_w):
        off = k - left
        if off == 0:
            xs = x
        elif off < 0:
            xs = jnp.where(row >= -off, pltpu.roll(x, -off, 0), 0.0)
        else:
            xs = jnp.where(row < n - off, pltpu.roll(x, n - off, 0), 0.0)
        y = y + xs * w_ref[k:k + 1, :]
    return y


def _log_sigmoid(z):
    return jnp.minimum(z, 0.0) - jnp.log1p(jnp.exp(-jnp.abs(z)))


def _lru_gates(xc_ref, n, row_chunk, wg_ref, bg_ref, quarter_c_ls, a_ref, b_ref, n_slabs):
    for r0 in range(0, n, row_chunk):
        for s in range(n_slabs):
            lanes = slice(s * LANES, (s + 1) * LANES)
            xh = xc_ref[r0:r0 + row_chunk, lanes]
            g = jnp.dot(xh.astype(BF16), wg_ref[s], preferred_element_type=F32)
            for d in range(2):
                tr = jnp.tanh(g[:, (2 * d) * LANES:(2 * d + 1) * LANES] + bg_ref[2 * d:2 * d + 1, lanes])
                ti = jnp.tanh(g[:, (2 * d + 1) * LANES:(2 * d + 2) * LANES] + bg_ref[2 * d + 1:2 * d + 2, lanes])
                hl = quarter_c_ls[d:d + 1, lanes]
                t = jnp.tanh(hl + hl * tr)
                q = pl.reciprocal(1.0 - t, full_range=False)
                nt = -t
                root = jnp.where(nt > 0.0, nt * lax.rsqrt(nt), 0.0)
                a_ref[d, s, r0:r0 + row_chunk, :] = (1.0 + t) * q
                b_ref[d, s, r0:r0 + row_chunk, :] = (root * q) * ((1.0 + ti) * xh)


def _lru_scan(a_ref, b_ref, sh_ref, sp_ref, n, h0, n_slabs, write_states):
    m = _scan_pitch(n)
    pairs = [(d, s) for d in range(2) for s in range(n_slabs)]
    sub = lax.broadcasted_iota(jnp.int32, (SUBLANES, LANES), 0)

    def row_of(d, j):
        return j if d == 0 else m - 1 - j

    def local_body(j, carry):
        out = []
        for (d, s), (h, p) in zip(pairs, carry):
            jj = row_of(d, j)
            a = a_ref[d, s, pl.ds(jj, SUBLANES, stride=m), :]
            b = b_ref[d, s, pl.ds(jj, SUBLANES, stride=m), :]
            h = a * h + b
            p = a * p
            if write_states:
                dst = pl.ds(pl.multiple_of(jj * SUBLANES, SUBLANES), SUBLANES)
                sh_ref[d, s, dst, :] = h
                sp_ref[d, s, dst, :] = p
            out.append((h, p))
        return tuple(out)

    init = tuple((jnp.zeros((SUBLANES, LANES), F32), jnp.ones((SUBLANES, LANES), F32)) for _ in pairs)
    local = lax.fori_loop(0, m, local_body, init, unroll=SCAN_UNROLL)

    carries, finals = [], []
    for (d, s), (hfin, ptot) in zip(pairs, local):
        first, last, shift = (0, SUBLANES - 1, 1) if d == 0 else (SUBLANES - 1, 0, SUBLANES - 1)
        hin = jnp.where(sub == first, jnp.broadcast_to(h0[d][s], (SUBLANES, LANES)), 0.0)
        for step in range(1, SUBLANES):
            k = first + step if d == 0 else first - step
            nxt = pltpu.roll(hfin + ptot * hin, shift, 0)
            hin = jnp.where(sub == k, nxt, hin)
        carries.append(hin)
        finals.append((hfin + ptot * hin)[last:last + 1, :])

    if write_states:
        def fix_body(j, _):
            for (d, s), hin in zip(pairs, carries):
                jj = row_of(d, j)
                src = pl.ds(pl.multiple_of(jj * SUBLANES, SUBLANES), SUBLANES)
                b_ref[d, s, pl.ds(jj, SUBLANES, stride=m), :] = sh_ref[d, s, src, :] + sp_ref[d, s, src, :] * hin
            return 0

        lax.fori_loop(0, m, fix_body, 0, unroll=SCAN_UNROLL)

    return [[finals[d * n_slabs + s] for s in range(n_slabs)] for d in range(2)]


def _lru_kernel(ux_ref, uxc_ref, ug_ref, cw_ref, cb_ref, wg_ref, bg_ref, lam_ref, o_ref,
                xc_s, xcc_s, a_s, b_s, ac_s, bc_s, sh_s, sp_s):
    n = ux_ref.shape[1]
    n_ctx = uxc_ref.shape[1]
    n_slabs = ux_ref.shape[2] // LANES
    ls = (0.25 * LRU_C) * _log_sigmoid(lam_ref[...])

    for ref, rows in ((a_s, n), (ac_s, n_ctx)):
        ref[:, :, rows:, :] = jnp.ones((2, n_slabs, ref.shape[2] - rows, LANES), F32)
    for ref, rows in ((b_s, n), (bc_s, n_ctx)):
        ref[:, :, rows:, :] = jnp.zeros((2, n_slabs, ref.shape[2] - rows, LANES), F32)

    left = LRU_CONV // 2
    xcc_s[...] = _depthwise_conv_rows(uxc_ref[0], cw_ref, cb_ref, left)
    xc_s[...] = _depthwise_conv_rows(ux_ref[0], cw_ref, cb_ref, left)

    _lru_gates(xcc_s, n_ctx, n_ctx, wg_ref, bg_ref, ls, ac_s, bc_s, n_slabs)
    _lru_gates(xc_s, n, LRU_ROW_CHUNK, wg_ref, bg_ref, ls, a_s, b_s, n_slabs)

    zero = jnp.zeros((1, LANES), F32)
    h0 = _lru_scan(ac_s, bc_s, None, None, n_ctx, [[zero] * n_slabs] * 2, n_slabs, False)
    _lru_scan(a_s, b_s, sh_s, sp_s, n, h0, n_slabs, True)

    for r0 in range(0, n, LRU_ROW_CHUNK):
        rows = slice(r0, r0 + LRU_ROW_CHUNK)
        for s in range(n_slabs):
            lanes = slice(s * LANES, (s + 1) * LANES)
            hsum = b_s[0, s, rows, :] + b_s[1, s, rows, :]
            gate = jax.nn.gelu(ug_ref[0, rows, lanes], approximate=True)
            o_ref[0, rows, lanes] = (hsum * gate).astype(o_ref.dtype)


def _rg_lru(ux, uxc, ug, conv_w, conv_b, wg, bg, lam):
    bsz, n, c = ux.shape
    n_ctx = uxc.shape[1]
    cb = LRU_CB
    n_slabs = cb // LANES
    lp = SUBLANES * _scan_pitch(n)
    cp = SUBLANES * _scan_pitch(n_ctx)
    seq = lambda rows: pl.BlockSpec((1, rows, cb), lambda b, j: (b, 0, j))
    par = lambda rows: pl.BlockSpec((rows, cb), lambda b, j: (0, j))
    return pl.pallas_call(
        _lru_kernel,
        grid=(bsz, c // cb),
        in_specs=[
            seq(n), seq(n_ctx), seq(n),
            par(conv_w.shape[0]), par(1),
            pl.BlockSpec((n_slabs, LANES, 4 * LANES), lambda b, j: (j, 0, 0)),
            par(4), par(2),
        ],
        out_specs=seq(n),
        out_shape=jax.ShapeDtypeStruct((bsz, n, c), BF16),
        scratch_shapes=[
            pltpu.VMEM((n, cb), F32),
            pltpu.VMEM((n_ctx, cb), F32),
            pltpu.VMEM((2, n_slabs, lp, LANES), F32),
            pltpu.VMEM((2, n_slabs, lp, LANES), F32),
            pltpu.VMEM((2, n_slabs, cp, LANES), F32),
            pltpu.VMEM((2, n_slabs, cp, LANES), F32),
            pltpu.VMEM((2, n_slabs, lp, LANES), F32),
            pltpu.VMEM((2, n_slabs, lp, LANES), F32),
        ],
        compiler_params=_cparams(("parallel", "parallel")),
        name="rg_lru",
    )(ux, uxc, ug, conv_w, conv_b.reshape(1, c), wg, bg, lam)


FFT_RADIX = 8
_FFT_BLOCK_OF = {0: (0, 0.0), 4: (4, 0.0), 2: (2, -1.0), 6: (2, 1.0), 1: (1, -1.0), 7: (1, 1.0), 3: (3, 1.0), 5: (3, -1.0)}


def _fourier_kernel(u_ref, m_ref, ccs_ref, wf_ref, bf_ref, o_ref, y_s):
    n = u_ref.shape[1]
    r = n // FFT_RADIX
    groups, gw, _ = wf_ref.shape
    x = [u_ref[0, p * r:(p + 1) * r, :].astype(F32) for p in range(FFT_RADIX)]
    e0, e1, e2, e3 = x[0] + x[4], x[0] - x[4], x[2] + x[6], x[2] - x[6]
    o0, o1, o2, o3 = x[1] + x[5], x[1] - x[5], x[3] + x[7], x[3] - x[7]
    ee, oo = e0 + e2, o0 + o2
    c = math.sqrt(0.5)
    p, m = c * (o1 - o3), c * (o1 + o3)
    t_re = {0: ee + oo, 4: ee - oo, 2: e0 - e2, 1: e1 + p, 3: e1 - p}
    t_im = {2: o0 - o2, 1: m + e3, 3: e3 - m}

    for k1 in range(FFT_RADIX):
        blk = _FFT_BLOCK_OF[k1][0]
        if blk in t_im:
            rhs = jnp.concatenate([t_re[blk], t_im[blk]], axis=0).astype(BF16)
            z = jnp.dot(m_ref[k1], rhs, preferred_element_type=F32)
        else:
            z = jnp.dot(m_ref[k1, :, :r], t_re[blk].astype(BF16), preferred_element_type=F32)
        for g in range(groups):
            cols = slice(g * gw, (g + 1) * gw)
            f = jnp.dot(z[:r, cols].astype(BF16), ccs_ref[0], preferred_element_type=F32)
            f = f + jnp.dot(z[r:, cols].astype(BF16), ccs_ref[1], preferred_element_type=F32)
            y = jnp.dot(f.astype(BF16), wf_ref[g], preferred_element_type=F32) + bf_ref[:, cols]
            for h in range(gw // LANES):
                slab = g * (gw // LANES) + h
                y_s[slab, pl.ds(k1, r, stride=FFT_RADIX), :] = y[:, h * LANES:(h + 1) * LANES]

    for slab in range(y_s.shape[0]):
        o_ref[0, :, slab * LANES:(slab + 1) * LANES] = y_s[slab].astype(o_ref.dtype)


def _fourier_tables(n, gw):
    r = n // FFT_RADIX
    k = jnp.arange(FFT_RADIX, dtype=jnp.int32)[:, None, None] + FFT_RADIX * jnp.arange(r, dtype=jnp.int32)[None, :, None]
    p2 = jnp.arange(r, dtype=jnp.int32)[None, None, :]
    ang = ((k * p2) % n).astype(F32) * (2.0 * math.pi / n)
    cn, sn = jnp.cos(ang) * n ** -0.5, jnp.sin(ang) * n ** -0.5
    sign = jnp.array([_FFT_BLOCK_OF[k1][1] for k1 in range(FFT_RADIX)], F32)[:, None, None]
    m = jnp.concatenate([jnp.concatenate([cn, sign * sn], axis=2),
                         jnp.concatenate([-sn, sign * cn], axis=2)], axis=1)
    idx = jnp.arange(gw, dtype=jnp.int32)
    ang_c = ((idx[:, None] * idx[None, :]) % gw).astype(F32) * (2.0 * math.pi / gw)
    ccs = jnp.stack([jnp.cos(ang_c), jnp.sin(ang_c)]) * gw ** -0.5
    return m.astype(BF16), ccs.astype(BF16)


def _fourier(uf, w_f, b_f):
    bsz, n, c = uf.shape
    groups, gw, _ = w_f.shape
    assert n % FFT_RADIX == 0 and gw % LANES == 0
    m, ccs = _fourier_tables(n, gw)
    return pl.pallas_call(
        _fourier_kernel,
        grid=(bsz,),
        in_specs=[
            pl.BlockSpec((1, n, c), lambda b: (b, 0, 0)),
            _const_spec(m.shape),
            _const_spec(ccs.shape),
            _const_spec(w_f.shape),
            _const_spec((1, c)),
        ],
        out_specs=pl.BlockSpec((1, n, c), lambda b: (b, 0, 0)),
        out_shape=jax.ShapeDtypeStruct((bsz, n, c), BF16),
        scratch_shapes=[pltpu.VMEM((c // LANES, n, LANES), F32)],
        compiler_params=_cparams(("parallel",)),
        name="fourier",
    )(uf, m, ccs, w_f.astype(BF16), b_f.reshape(1, c))


def _out_proj_kernel(yf_ref, yl_ref, x_ref, prow_ref, pcol_ref, mod_ref, gpost_ref, gpre_ref, w_ref,
                     x1_ref, h2_ref):
    k = yf_ref.shape[2]
    y = jnp.dot(yf_ref[0], w_ref[:k, :], preferred_element_type=F32)
    y = y + jnp.dot(yl_ref[0], w_ref[k:, :], preferred_element_type=F32)
    xp = x_ref[0] + _tile_pos_embed(prow_ref, pcol_ref)
    x1 = xp + mod_ref[0, 2:3, :] * _rms_norm(y, gpost_ref[...])
    x1_ref[0] = x1
    h2 = _rms_norm(x1, gpre_ref[...]) * (1.0 + mod_ref[0, 4:5, :]) + mod_ref[0, 3:4, :]
    h2_ref[0] = h2.astype(h2_ref.dtype)


def _out_proj(yf, yl, x, pe, mod3, g_post, g_pre, w_out):
    bsz, n, d = x.shape
    k = yf.shape[2]
    tm = IN_TM
    act = pl.BlockSpec((1, tm, k), lambda i, b: (b, i, 0))
    row = pl.BlockSpec((1, tm, d), lambda i, b: (b, i, 0))
    return pl.pallas_call(
        _out_proj_kernel,
        grid=(n // tm, bsz),
        in_specs=[
            act, act, row,
            *_pos_embed_specs(pe, tm),
            pl.BlockSpec((1, N_MOD, d), lambda i, b: (b, 0, 0)),
            _const_spec((1, d)), _const_spec((1, d)),
            _const_spec(w_out.shape),
        ],
        out_specs=[row, row],
        out_shape=[jax.ShapeDtypeStruct((bsz, n, d), F32), jax.ShapeDtypeStruct((bsz, n, d), BF16)],
        compiler_params=_cparams(("parallel", "parallel")),
        name="out_proj",
    )(yf, yl, x, *pe, mod3, g_post.reshape(1, d), g_pre.reshape(1, d), w_out)


def _ffn_kernel(h_ref, hprev_ref, hnext_ref, wg_ref, wv_ref, cwg_ref, cwv_ref, cbg_ref, cbv_ref,
                wd_ref, x1_ref, mod_ref, gpost_ref, o_ref, lhs_s):
    i = pl.program_id(1)
    c = pl.program_id(2)
    tm = h_ref.shape[1]
    halo = hprev_ref.shape[1]
    rows = tm + 2 * halo

    @pl.when(c == 0)
    def _():
        lhs_s[0:halo, :] = jnp.where(i > 0, hprev_ref[0], jnp.zeros_like(hprev_ref[0]))
        lhs_s[halo:halo + tm, :] = h_ref[0]
        lhs_s[halo + tm:, :] = jnp.where(i < pl.num_programs(1) - 1, hnext_ref[0], jnp.zeros_like(hnext_ref[0]))
        o_ref[0] = jnp.zeros(o_ref.shape[1:], F32)

    def conv_branch(w_ref, cw_ref, cb_ref):
        up = jnp.dot(lhs_s[...], w_ref[...], preferred_element_type=F32)
        left = FFN_CONV // 2
        y = cb_ref[...] + jnp.zeros((tm, up.shape[1]), F32)
        for k in range(FFN_CONV):
            off = k - left
            shifted = up if off == 0 else pltpu.roll(up, (-off) % rows, 0)
            y = y + shifted[halo:halo + tm, :] * cw_ref[k:k + 1, :]
        return y

    g = conv_branch(wg_ref, cwg_ref, cbg_ref)
    v = conv_branch(wv_ref, cwv_ref, cbv_ref)
    act = (jax.nn.gelu(g, approximate=True) * v).astype(BF16)
    o_ref[0] += jnp.dot(act, wd_ref[...], preferred_element_type=F32)

    @pl.when(c == pl.num_programs(2) - 1)
    def _():
        o_ref[0] = x1_ref[0] + mod_ref[0, 5:6, :] * _rms_norm(o_ref[0], gpost_ref[...])


def _conv_ffn(h2, x1, mod3, g_post, w_up, conv_w, conv_b, w_down):
    bsz, n, d = h2.shape
    d_ff = w_down.shape[0]
    tm, tf, halo = FFN_TM, FFN_TF, FFN_HALO
    nt, nf = n // tm, d_ff // tf
    hb = tm // halo
    row = pl.BlockSpec((1, tm, d), lambda b, i, c: (b, i, 0))
    row_once = pl.BlockSpec((1, tm, d), lambda b, i, c: (b, i, 0), pipeline_mode=pl.Buffered(1))
    gcol = lambda r: pl.BlockSpec((r, tf), lambda b, i, c: (0, c))
    vcol = lambda r: pl.BlockSpec((r, tf), lambda b, i, c: (0, c + nf))
    conv_b2 = conv_b.reshape(1, 2 * d_ff)
    return pl.pallas_call(
        _ffn_kernel,
        grid=(bsz, nt, nf),
        in_specs=[
            row_once,
            pl.BlockSpec((1, halo, d), lambda b, i, c: (b, jnp.maximum(i * hb - 1, 0), 0)),
            pl.BlockSpec((1, halo, d), lambda b, i, c: (b, jnp.minimum((i + 1) * hb, n // halo - 1), 0)),
            gcol(d), vcol(d),
            gcol(FFN_CONV), vcol(FFN_CONV),
            gcol(1), vcol(1),
            pl.BlockSpec((tf, d), lambda b, i, c: (c, 0)),
            row_once,
            pl.BlockSpec((1, N_MOD, d), lambda b, i, c: (b, 0, 0)),
            _const_spec((1, d)),
        ],
        out_specs=row,
        out_shape=jax.ShapeDtypeStruct((bsz, n, d), F32),
        scratch_shapes=[pltpu.VMEM((tm + 2 * halo, d), BF16)],
        compiler_params=_cparams(("parallel", "parallel", "arbitrary"), FFN_VMEM_LIMIT_BYTES),
        name="conv_ffn",
    )(h2, h2, h2, w_up, w_up, conv_w, conv_w, conv_b2, conv_b2, w_down, x1, mod3, g_post.reshape(1, d))


def _grid_pos_tables(n_tokens, d_model):
    quarter = d_model // 4
    freqs = POS_BASE ** (-jnp.arange(quarter, dtype=F32) / quarter)

    def enc(count):
        ang = jnp.arange(count, dtype=F32)[:, None] * freqs[None, :]
        return jnp.concatenate([jnp.sin(ang), jnp.cos(ang)], axis=-1)

    return enc(n_tokens // GRID_W), enc(GRID_W)


def kernel(x, c, ctx, c_ctx, w_ada, b_ada, g_mix_pre, g_mix_post, g_ffn_pre, g_ffn_post,
           w_in, conv_lru_w, conv_lru_b, w_rec_gate, b_rec_gate, w_in_gate, b_in_gate,
           lru_lambda, w_fourier, b_fourier, w_out, w_up, conv_ffn_w, conv_ffn_b, w_down):
    bsz, n, d = x.shape
    d_fourier = w_fourier.shape[1] * w_fourier.shape[2]
    d_lru = conv_lru_w.shape[2]
    assert w_ada.shape[0] == 1, "single layer: the context stream is only read, never updated"
    assert bsz + 1 <= MOD_ROWS and d_lru // N_LRU_HEADS == LANES
    l = 0
    pe = _grid_pos_tables(n, d)

    cond = jnp.zeros((MOD_ROWS, d), F32).at[:bsz].set(c).at[bsz].set(c_ctx)
    mod3 = _modulation(cond, w_ada[l], b_ada[l]).reshape(MOD_ROWS, N_MOD, d)

    w_in_b = w_in[l].astype(BF16)
    x_cols = (d_fourier, d_fourier + d_lru)
    (uxc,) = _in_proj(ctx, None, mod3, lambda b: bsz, g_mix_pre[l], w_in_b[:, x_cols[0]:x_cols[1]],
                      ((0, d_lru),), (F32,), ctx.shape[1], "in_proj_ctx")
    uf, ux, ug = _in_proj(x, pe, mod3, lambda b: b, g_mix_pre[l], w_in_b,
                          ((0, d_fourier), x_cols, (x_cols[1], x_cols[1] + d_lru)),
                          (BF16, F32, F32), IN_TM, "in_proj")

    wg = (0.5 * jnp.concatenate([w_rec_gate[l, 0], w_in_gate[l, 0], w_rec_gate[l, 1], w_in_gate[l, 1]],
                                axis=-1)).astype(BF16)
    bg = 0.5 * jnp.stack([b_rec_gate[l, 0], b_in_gate[l, 0], b_rec_gate[l, 1], b_in_gate[l, 1]])
    y_lru = _rg_lru(ux, uxc, ug, conv_lru_w[l], conv_lru_b[l], wg, bg, lru_lambda[l])
    y_fourier = _fourier(uf, w_fourier[l], b_fourier[l])

    x1, h2 = _out_proj(y_fourier, y_lru, x, pe, mod3, g_mix_post[l], g_ffn_pre[l], w_out[l].astype(BF16))
    return _conv_ffn(h2, x1, mod3, g_ffn_post[l], w_up[l].astype(BF16), conv_ffn_w[l], conv_ffn_b[l],
                     w_down[l].astype(BF16))
```

```python
import functools
import math

import jax
import jax.numpy as jnp
from jax import lax
from jax.experimental import pallas as pl
from jax.experimental.pallas import tpu as pltpu

F32 = jnp.float32
BF16 = jnp.bfloat16

EPS = 1e-6
POS_BASE = 10000.0
GRID_W = 64
N_MOD = 6
N_FOURIER_GROUPS = 4
N_LRU_HEADS = 8
LRU_CONV = 4
LRU_C = 8.0
FFN_CONV = 3

LANES = 128
SUBLANES = 8
VMEM_BYTES = 64 * 1024 * 1024
VMEM_LIMIT_BYTES = 60000 * 1024
FFN_VMEM_LIMIT_BYTES = VMEM_BYTES - 2 * 1024 * 1024

MOD_ROWS = 16
IN_TM = 512
LRU_CB = 256
LRU_ROW_CHUNK = 512
SCAN_UNROLL = 4
FFN_TM = 1024
FFN_TF = 512
FFN_HALO = 16
FFN_EPILOGUE_ROWS = 256


def _cparams(sem, vmem_limit_bytes=VMEM_LIMIT_BYTES):
    return pltpu.CompilerParams(dimension_semantics=sem, vmem_limit_bytes=vmem_limit_bytes)


def _const_spec(shape):
    nd = len(shape)
    return pl.BlockSpec(shape, lambda *_: (0,) * nd, pipeline_mode=pl.Buffered(1))


def _mod_kernel(c_ref, w_ref, b_ref, o_ref):
    cv = c_ref[...]
    s = (cv * jax.nn.sigmoid(cv)).astype(BF16)
    o_ref[...] = jnp.dot(s, w_ref[...].astype(BF16), preferred_element_type=F32) + b_ref[...]


def _modulation(cond, w_ada, b_ada):
    d = cond.shape[1]
    n = w_ada.shape[1]
    tn = 1024
    return pl.pallas_call(
        _mod_kernel,
        grid=(n // tn,),
        in_specs=[
            pl.BlockSpec((MOD_ROWS, d), lambda j: (0, 0)),
            pl.BlockSpec((d, tn), lambda j: (0, j)),
            pl.BlockSpec((1, tn), lambda j: (0, j)),
        ],
        out_specs=pl.BlockSpec((MOD_ROWS, tn), lambda j: (0, j)),
        out_shape=jax.ShapeDtypeStruct((MOD_ROWS, n), F32),
        compiler_params=_cparams(("parallel",)),
        name="modulation",
    )(cond, w_ada, b_ada.reshape(1, n))


def _rms_norm(v, g):
    ms = jnp.mean(v * v, axis=-1, keepdims=True)
    return v * lax.rsqrt(ms + EPS) * g


def _tile_pos_embed(prow_ref, pcol_ref):
    rows = jnp.concatenate([jnp.broadcast_to(prow_ref[r:r + 1, :], pcol_ref.shape)
                            for r in range(prow_ref.shape[0])], axis=0)
    cols = jnp.concatenate([pcol_ref[...]] * prow_ref.shape[0], axis=0)
    return jnp.concatenate([rows, cols], axis=1)


def _in_proj_kernel(*refs, has_pe, col_splits):
    if has_pe:
        x_ref, prow_ref, pcol_ref, mod_ref, g_ref, w_ref = refs[:6]
        out_refs = refs[6:]
        xp = x_ref[0] + _tile_pos_embed(prow_ref, pcol_ref)
    else:
        x_ref, mod_ref, g_ref, w_ref = refs[:4]
        out_refs = refs[4:]
        xp = x_ref[0]
    h = _rms_norm(xp, g_ref[...]) * (1.0 + mod_ref[0, 1:2, :]) + mod_ref[0, 0:1, :]
    hb = h.astype(BF16)
    for o_ref, (c0, c1) in zip(out_refs, col_splits):
        o_ref[0] = jnp.dot(hb, w_ref[:, c0:c1], preferred_element_type=F32).astype(o_ref.dtype)


def _pos_embed_specs(pe, tm):
    prow, pcol = pe
    assert tm % GRID_W == 0
    return [pl.BlockSpec((tm // GRID_W, prow.shape[1]), lambda i, b: (i, 0)), _const_spec(pcol.shape)]


def _in_proj(x, pe, mod3, mod_row, g, w, col_splits, out_dtypes, tm, name):
    bsz, n, d = x.shape
    has_pe = pe is not None
    in_specs = [pl.BlockSpec((1, tm, d), lambda i, b: (b, i, 0))]
    args = [x]
    if has_pe:
        in_specs += _pos_embed_specs(pe, tm)
        args += list(pe)
    in_specs += [
        pl.BlockSpec((1, N_MOD, d), lambda i, b: (mod_row(b), 0, 0)),
        _const_spec((1, d)),
        _const_spec(w.shape),
    ]
    args += [mod3, g.reshape(1, d), w]
    out_specs = [pl.BlockSpec((1, tm, c1 - c0), lambda i, b: (b, i, 0)) for c0, c1 in col_splits]
    out_shape = [jax.ShapeDtypeStruct((bsz, n, c1 - c0), dt) for (c0, c1), dt in zip(col_splits, out_dtypes)]
    return pl.pallas_call(
        functools.partial(_in_proj_kernel, has_pe=has_pe, col_splits=col_splits),
        grid=(n // tm, bsz),
        in_specs=in_specs,
        out_specs=out_specs,
        out_shape=out_shape,
        compiler_params=_cparams(("parallel", "parallel")),
        name=name,
    )(*args)


def _scan_pitch(n):
    m = -(-n // SUBLANES)
    while m % 8 != 4:
        m += 1
    return m


def _depthwise_conv_rows(x, w_ref, b_ref, left):
    n = x.shape[0]
    k_w = w_ref.shape[0]
    row = lax.broadcasted_iota(jnp.int32, x.shape, 0)
    y = b_ref[...] + jnp.zeros_like(x)
    for k in range(k_w):
        off = k - left
        if off == 0:
            xs = x
        elif off < 0:
            xs = jnp.where(row >= -off, pltpu.roll(x, -off, 0), 0.0)
        else:
            xs = jnp.where(row < n - off, pltpu.roll(x, n - off, 0), 0.0)
        y = y + xs * w_ref[k:k + 1, :]
    return y


def _log_sigmoid(z):
    return jnp.minimum(z, 0.0) - jnp.log1p(jnp.exp(-jnp.abs(z)))


def _lru_gates(xc_ref, n, row_chunk, wg_ref, bg_ref, quarter_c_ls, a_ref, b_ref, n_slabs):
    for r0 in range(0, n, row_chunk):
        for s in range(n_slabs):
            lanes = slice(s * LANES, (s + 1) * LANES)
            xh = xc_ref[r0:r0 + row_chunk, lanes]
            g = jnp.dot(xh.astype(BF16), wg_ref[s], preferred_element_type=F32)
            for d in range(2):
                tr = jnp.tanh(g[:, (2 * d) * LANES:(2 * d + 1) * LANES] + bg_ref[2 * d:2 * d + 1, lanes])
                ti = jnp.tanh(g[:, (2 * d + 1) * LANES:(2 * d + 2) * LANES] + bg_ref[2 * d + 1:2 * d + 2, lanes])
                hl = quarter_c_ls[d:d + 1, lanes]
                t = jnp.tanh(hl + hl * tr)
                q = pl.reciprocal(1.0 - t, full_range=False)
                nt = -t
                root = jnp.where(nt > 0.0, nt * lax.rsqrt(nt), 0.0)
                a_ref[d, s, r0:r0 + row_chunk, :] = (1.0 + t) * q
                b_ref[d, s, r0:r0 + row_chunk, :] = (root * q) * ((1.0 + ti) * xh)


def _lru_scan(a_ref, b_ref, sh_ref, sp_ref, n, h0, n_slabs, write_states):
    m = _scan_pitch(n)
    pairs = [(d, s) for d in range(2) for s in range(n_slabs)]
    sub = lax.broadcasted_iota(jnp.int32, (SUBLANES, LANES), 0)

    def row_of(d, j):
        return j if d == 0 else m - 1 - j

    def local_body(j, carry):
        out = []
        for (d, s), (h, p) in zip(pairs, carry):
            jj = row_of(d, j)
            a = a_ref[d, s, pl.ds(jj, SUBLANES, stride=m), :]
            b = b_ref[d, s, pl.ds(jj, SUBLANES, stride=m), :]
            h = a * h + b
            p = a * p
            if write_states:
                dst = pl.ds(pl.multiple_of(jj * SUBLANES, SUBLANES), SUBLANES)
                sh_ref[d, s, dst, :] = h
                sp_ref[d, s, dst, :] = p
            out.append((h, p))
        return tuple(out)

    init = tuple((jnp.zeros((SUBLANES, LANES), F32), jnp.ones((SUBLANES, LANES), F32)) for _ in pairs)
    local = lax.fori_loop(0, m, local_body, init, unroll=SCAN_UNROLL)

    carries, finals = [], []
    for (d, s), (hfin, ptot) in zip(pairs, local):
        first, last, shift = (0, SUBLANES - 1, 1) if d == 0 else (SUBLANES - 1, 0, SUBLANES - 1)
        hin = jnp.where(sub == first, jnp.broadcast_to(h0[d][s], (SUBLANES, LANES)), 0.0)
        for step in range(1, SUBLANES):
            k = first + step if d == 0 else first - step
            nxt = pltpu.roll(hfin + ptot * hin, shift, 0)
            hin = jnp.where(sub == k, nxt, hin)
        carries.append(hin)
        finals.append((hfin + ptot * hin)[last:last + 1, :])

    if write_states:
        def fix_body(j, _):
            for (d, s), hin in zip(pairs, carries):
                jj = row_of(d, j)
                src = pl.ds(pl.multiple_of(jj * SUBLANES, SUBLANES), SUBLANES)
                b_ref[d, s, pl.ds(jj, SUBLANES, stride=m), :] = sh_ref[d, s, src, :] + sp_ref[d, s, src, :] * hin
            return 0

        lax.fori_loop(0, m, fix_body, 0, unroll=SCAN_UNROLL)

    return [[finals[d * n_slabs + s] for s in range(n_slabs)] for d in range(2)]


def _lru_kernel(ux_ref, uxc_ref, ug_ref, cw_ref, cb_ref, wg_ref, bg_ref, lam_ref, o_ref,
                xc_s, xcc_s, a_s, b_s, ac_s, bc_s, sh_s, sp_s):
    n = ux_ref.shape[1]
    n_ctx = uxc_ref.shape[1]
    n_slabs = ux_ref.shape[2] // LANES
    ls = (0.25 * LRU_C) * _log_sigmoid(lam_ref[...])

    for ref, rows in ((a_s, n), (ac_s, n_ctx)):
        ref[:, :, rows:, :] = jnp.ones((2, n_slabs, ref.shape[2] - rows, LANES), F32)
    for ref, rows in ((b_s, n), (bc_s, n_ctx)):
        ref[:, :, rows:, :] = jnp.zeros((2, n_slabs, ref.shape[2] - rows, LANES), F32)

    left = LRU_CONV // 2
    xcc_s[...] = _depthwise_conv_rows(uxc_ref[0], cw_ref, cb_ref, left)
    xc_s[...] = _depthwise_conv_rows(ux_ref[0], cw_ref, cb_ref, left)

    _lru_gates(xcc_s, n_ctx, n_ctx, wg_ref, bg_ref, ls, ac_s, bc_s, n_slabs)
    _lru_gates(xc_s, n, LRU_ROW_CHUNK, wg_ref, bg_ref, ls, a_s, b_s, n_slabs)

    zero = jnp.zeros((1, LANES), F32)
    h0 = _lru_scan(ac_s, bc_s, None, None, n_ctx, [[zero] * n_slabs] * 2, n_slabs, False)
    _lru_scan(a_s, b_s, sh_s, sp_s, n, h0, n_slabs, True)

    for r0 in range(0, n, LRU_ROW_CHUNK):
        rows = slice(r0, r0 + LRU_ROW_CHUNK)
        for s in range(n_slabs):
            lanes = slice(s * LANES, (s + 1) * LANES)
            hsum = b_s[0, s, rows, :] + b_s[1, s, rows, :]
            gate = jax.nn.gelu(ug_ref[0, rows, lanes], approximate=True)
            o_ref[0, rows, lanes] = (hsum * gate).astype(o_ref.dtype)


def _rg_lru(ux, uxc, ug, conv_w, conv_b, wg, bg, lam):
    bsz, n, c = ux.shape
    n_ctx = uxc.shape[1]
    cb = LRU_CB
    n_slabs = cb // LANES
    lp = SUBLANES * _scan_pitch(n)
    cp = SUBLANES * _scan_pitch(n_ctx)
    seq = lambda rows: pl.BlockSpec((1, rows, cb), lambda b, j: (b, 0, j))
    par = lambda rows: pl.BlockSpec((rows, cb), lambda b, j: (0, j))
    return pl.pallas_call(
        _lru_kernel,
        grid=(bsz, c // cb),
        in_specs=[
            seq(n), seq(n_ctx), seq(n),
            par(conv_w.shape[0]), par(1),
            pl.BlockSpec((n_slabs, LANES, 4 * LANES), lambda b, j: (j, 0, 0)),
            par(4), par(2),
        ],
        out_specs=seq(n),
        out_shape=jax.ShapeDtypeStruct((bsz, n, c), BF16),
        scratch_shapes=[
            pltpu.VMEM((n, cb), F32),
            pltpu.VMEM((n_ctx, cb), F32),
            pltpu.VMEM((2, n_slabs, lp, LANES), F32),
            pltpu.VMEM((2, n_slabs, lp, LANES), F32),
            pltpu.VMEM((2, n_slabs, cp, LANES), F32),
            pltpu.VMEM((2, n_slabs, cp, LANES), F32),
            pltpu.VMEM((2, n_slabs, lp, LANES), F32),
            pltpu.VMEM((2, n_slabs, lp, LANES), F32),
        ],
        compiler_params=_cparams(("parallel", "parallel")),
        name="rg_lru",
    )(ux, uxc, ug, conv_w, conv_b.reshape(1, c), wg, bg, lam)


FFT_RADIX = 8
_FFT_BLOCK_OF = {0: (0, 0.0), 4: (4, 0.0), 2: (2, -1.0), 6: (2, 1.0), 1: (1, -1.0), 7: (1, 1.0), 3: (3, 1.0), 5: (3, -1.0)}


def _fourier_kernel(u_ref, m_ref, ccs_ref, wf_ref, bf_ref, o_ref, y_s):
    n = u_ref.shape[1]
    r = n // FFT_RADIX
    groups, gw, _ = wf_ref.shape
    x = [u_ref[0, p * r:(p + 1) * r, :].astype(F32) for p in range(FFT_RADIX)]
    e0, e1, e2, e3 = x[0] + x[4], x[0] - x[4], x[2] + x[6], x[2] - x[6]
    o0, o1, o2, o3 = x[1] + x[5], x[1] - x[5], x[3] + x[7], x[3] - x[7]
    ee, oo = e0 + e2, o0 + o2
    c = math.sqrt(0.5)
    p, m = c * (o1 - o3), c * (o1 + o3)
    t_re = {0: ee + oo, 4: ee - oo, 2: e0 - e2, 1: e1 + p, 3: e1 - p}
    t_im = {2: o0 - o2, 1: m + e3, 3: e3 - m}

    for k1 in range(FFT_RADIX):
        blk = _FFT_BLOCK_OF[k1][0]
        if blk in t_im:
            rhs = jnp.concatenate([t_re[blk], t_im[blk]], axis=0).astype(BF16)
            z = jnp.dot(m_ref[k1], rhs, preferred_element_type=F32)
        else:
            z = jnp.dot(m_ref[k1, :, :r], t_re[blk].astype(BF16), preferred_element_type=F32)
        for g in range(groups):
            cols = slice(g * gw, (g + 1) * gw)
            f = jnp.dot(z[:r, cols].astype(BF16), ccs_ref[0], preferred_element_type=F32)
            f = f + jnp.dot(z[r:, cols].astype(BF16), ccs_ref[1], preferred_element_type=F32)
            y = jnp.dot(f.astype(BF16), wf_ref[g], preferred_element_type=F32) + bf_ref[:, cols]
            for h in range(gw // LANES):
                slab = g * (gw // LANES) + h
                y_s[slab, pl.ds(k1, r, stride=FFT_RADIX), :] = y[:, h * LANES:(h + 1) * LANES]

    for slab in range(y_s.shape[0]):
        o_ref[0, :, slab * LANES:(slab + 1) * LANES] = y_s[slab].astype(o_ref.dtype)


def _fourier_tables(n, gw):
    r = n // FFT_RADIX
    k = jnp.arange(FFT_RADIX, dtype=jnp.int32)[:, None, None] + FFT_RADIX * jnp.arange(r, dtype=jnp.int32)[None, :, None]
    p2 = jnp.arange(r, dtype=jnp.int32)[None, None, :]
    ang = ((k * p2) % n).astype(F32) * (2.0 * math.pi / n)
    cn, sn = jnp.cos(ang) * n ** -0.5, jnp.sin(ang) * n ** -0.5
    sign = jnp.array([_FFT_BLOCK_OF[k1][1] for k1 in range(FFT_RADIX)], F32)[:, None, None]
    m = jnp.concatenate([jnp.concatenate([cn, sign * sn], axis=2),
                         jnp.concatenate([-sn, sign * cn], axis=2)], axis=1)
    idx = jnp.arange(gw, dtype=jnp.int32)
    ang_c = ((idx[:, None] * idx[None, :]) % gw).astype(F32) * (2.0 * math.pi / gw)
    ccs = jnp.stack([jnp.cos(ang_c), jnp.sin(ang_c)]) * gw ** -0.5
    return m.astype(BF16), ccs.astype(BF16)


def _fourier(uf, w_f, b_f):
    bsz, n, c = uf.shape
    groups, gw, _ = w_f.shape
    assert n % FFT_RADIX == 0 and gw % LANES == 0
    m, ccs = _fourier_tables(n, gw)
    return pl.pallas_call(
        _fourier_kernel,
        grid=(bsz,),
        in_specs=[
            pl.BlockSpec((1, n, c), lambda b: (b, 0, 0)),
            _const_spec(m.shape),
            _const_spec(ccs.shape),
            _const_spec(w_f.shape),
            _const_spec((1, c)),
        ],
        out_specs=pl.BlockSpec((1, n, c), lambda b: (b, 0, 0)),
        out_shape=jax.ShapeDtypeStruct((bsz, n, c), BF16),
        scratch_shapes=[pltpu.VMEM((c // LANES, n, LANES), F32)],
        compiler_params=_cparams(("parallel",)),
        name="fourier",
    )(uf, m, ccs, w_f.astype(BF16), b_f.reshape(1, c))


def _out_proj_kernel(yf_ref, yl_ref, x_ref, prow_ref, pcol_ref, mod_ref, gpost_ref, gpre_ref, w_ref,
                     x1_ref, h2_ref):
    k = yf_ref.shape[2]
    y = jnp.dot(yf_ref[0], w_ref[:k, :], preferred_element_type=F32)
    y = y + jnp.dot(yl_ref[0], w_ref[k:, :], preferred_element_type=F32)
    xp = x_ref[0] + _tile_pos_embed(prow_ref, pcol_ref)
    x1 = xp + mod_ref[0, 2:3, :] * _rms_norm(y, gpost_ref[...])
    x1_ref[0] = x1
    h2 = _rms_norm(x1, gpre_ref[...]) * (1.0 + mod_ref[0, 4:5, :]) + mod_ref[0, 3:4, :]
    h2_ref[0] = h2.astype(h2_ref.dtype)


def _out_proj(yf, yl, x, pe, mod3, g_post, g_pre, w_out):
    bsz, n, d = x.shape
    k = yf.shape[2]
    tm = IN_TM
    act = pl.BlockSpec((1, tm, k), lambda i, b: (b, i, 0))
    row = pl.BlockSpec((1, tm, d), lambda i, b: (b, i, 0))
    return pl.pallas_call(
        _out_proj_kernel,
        grid=(n // tm, bsz),
        in_specs=[
            act, act, row,
            *_pos_embed_specs(pe, tm),
            pl.BlockSpec((1, N_MOD, d), lambda i, b: (b, 0, 0)),
            _const_spec((1, d)), _const_spec((1, d)),
            _const_spec(w_out.shape),
        ],
        out_specs=[row, row],
        out_shape=[jax.ShapeDtypeStruct((bsz, n, d), F32), jax.ShapeDtypeStruct((bsz, n, d), BF16)],
        compiler_params=_cparams(("parallel", "parallel")),
        name="out_proj",
    )(yf, yl, x, *pe, mod3, g_post.reshape(1, d), g_pre.reshape(1, d), w_out)


def _ffn_kernel(h_hbm, x1_hbm, wg_ref, wv_ref, cwg_ref, cwv_ref, cbg_ref, cbv_ref, wd_ref, mod_ref, gpost_ref,
                o_hbm, lhs_s, acc_s, xo_s, lhs_sem, x1_sem, out_sem, *, n_t, n_f):
    b, i, c = pl.program_id(0), pl.program_id(1), pl.program_id(2)
    tm = acc_s.shape[0]
    rows = lhs_s.shape[1]
    halo = (rows - tm) // 2
    t = b * n_t + i
    slot = t % 2
    last_tile = pl.num_programs(0) * n_t - 1

    def lhs_dma(bb, ii, sl, start):
        def run(src_row0, dst_row0, n_rows):
            cp = pltpu.make_async_copy(h_hbm.at[bb, pl.ds(src_row0, n_rows)],
                                       lhs_s.at[sl, pl.ds(dst_row0, n_rows)], lhs_sem.at[sl])
            if start:
                cp.start()
            else:
                cp.wait()

        pl.when(ii == 0)(lambda: run(0, halo, tm + halo))
        pl.when(ii == n_t - 1)(lambda: run((n_t - 1) * tm - halo, 0, tm + halo))
        if n_t > 2:
            pl.when(jnp.logical_and(ii > 0, ii < n_t - 1))(
                lambda: run(pl.multiple_of(ii * tm - halo, halo), 0, rows))

    def x1_copy():
        return pltpu.make_async_copy(x1_hbm.at[b, pl.ds(pl.multiple_of(i * tm, tm), tm)], xo_s, x1_sem.at[0])

    def out_copy(bb, ii):
        return pltpu.make_async_copy(xo_s, o_hbm.at[bb, pl.ds(pl.multiple_of(ii * tm, tm), tm)], out_sem.at[0])

    def conv_branch(w_ref, cw_ref, cb_ref):
        up = jnp.dot(lhs_s[slot], w_ref[...], preferred_element_type=F32)
        left = FFN_CONV // 2
        y = cb_ref[...] + jnp.zeros((tm, up.shape[1]), F32)
        for k in range(FFN_CONV):
            off = k - left
            shifted = up if off == 0 else pltpu.roll(up, (-off) % rows, 0)
            y = y + shifted[halo:halo + tm, :] * cw_ref[k:k + 1, :]
        return y

    @pl.when(c == 0)
    def _():
        pl.when(t == 0)(lambda: lhs_dma(b, i, slot, True))
        lhs_dma(b, i, slot, False)
        zeros = jnp.zeros((halo, lhs_s.shape[2]), lhs_s.dtype)

        @pl.when(i == 0)
        def _():
            lhs_s[slot, 0:halo, :] = zeros

        @pl.when(i == n_t - 1)
        def _():
            lhs_s[slot, halo + tm:, :] = zeros

        acc_s[...] = jnp.zeros_like(acc_s)

    @pl.when(c == 1)
    def _():
        nxt = i + 1 < n_t
        pl.when(t < last_tile)(
            lambda: lhs_dma(jnp.where(nxt, b, b + 1), jnp.where(nxt, i + 1, 0), 1 - slot, True))
        prv = i > 0
        pl.when(t > 0)(lambda: out_copy(jnp.where(prv, b, b - 1), jnp.where(prv, i - 1, n_t - 1)).wait())
        x1_copy().start()

    g = conv_branch(wg_ref, cwg_ref, cbg_ref)
    v = conv_branch(wv_ref, cwv_ref, cbv_ref)
    act = (jax.nn.gelu(g, approximate=True) * v).astype(BF16)
    acc_s[...] += jnp.dot(act, wd_ref[...], preferred_element_type=F32)

    @pl.when(c == n_f - 1)
    def _():
        x1_copy().wait()
        for r0 in range(0, tm, FFN_EPILOGUE_ROWS):
            rs = slice(r0, r0 + FFN_EPILOGUE_ROWS)
            xo_s[rs, :] = xo_s[rs, :] + mod_ref[0, 5:6, :] * _rms_norm(acc_s[rs, :], gpost_ref[...])
        out_copy(b, i).start()
        pl.when(t == last_tile)(lambda: out_copy(b, i).wait())


def _conv_ffn(h2, x1, mod3, g_post, w_up, conv_w, conv_b, w_down):
    bsz, n, d = h2.shape
    d_ff = w_down.shape[0]
    tm, tf, halo = FFN_TM, FFN_TF, FFN_HALO
    nt, nf = n // tm, d_ff // tf
    assert nt >= 2 and n % tm == 0 and d_ff % tf == 0 and tm % FFN_EPILOGUE_ROWS == 0
    gcol = lambda r: pl.BlockSpec((r, tf), lambda b, i, c: (0, c))
    vcol = lambda r: pl.BlockSpec((r, tf), lambda b, i, c: (0, c + nf))
    any_spec = pl.BlockSpec(memory_space=pl.ANY)
    conv_b2 = conv_b.reshape(1, 2 * d_ff)
    rows = tm + 2 * halo
    return pl.pallas_call(
        functools.partial(_ffn_kernel, n_t=nt, n_f=nf),
        grid=(bsz, nt, nf),
        in_specs=[
            any_spec, any_spec,
            gcol(d), vcol(d),
            gcol(FFN_CONV), vcol(FFN_CONV),
            gcol(1), vcol(1),
            pl.BlockSpec((tf, d), lambda b, i, c: (c, 0)),
            pl.BlockSpec((1, N_MOD, d), lambda b, i, c: (b, 0, 0)),
            _const_spec((1, d)),
        ],
        out_specs=any_spec,
        out_shape=jax.ShapeDtypeStruct((bsz, n, d), F32),
        scratch_shapes=[
            pltpu.VMEM((2, rows, d), BF16),
            pltpu.VMEM((tm, d), F32),
            pltpu.VMEM((tm, d), F32),
            pltpu.SemaphoreType.DMA((2,)),
            pltpu.SemaphoreType.DMA((1,)),
            pltpu.SemaphoreType.DMA((1,)),
        ],
        compiler_params=_cparams(("arbitrary", "arbitrary", "arbitrary"), FFN_VMEM_LIMIT_BYTES),
        name="conv_ffn",
    )(h2, x1, w_up, w_up, conv_w, conv_w, conv_b2, conv_b2, w_down, mod3, g_post.reshape(1, d))


def _grid_pos_tables(n_tokens, d_model):
    quarter = d_model // 4
    freqs = POS_BASE ** (-jnp.arange(quarter, dtype=F32) / quarter)

    def enc(count):
        ang = jnp.arange(count, dtype=F32)[:, None] * freqs[None, :]
        return jnp.concatenate([jnp.sin(ang), jnp.cos(ang)], axis=-1)

    return enc(n_tokens // GRID_W), enc(GRID_W)


def kernel(x, c, ctx, c_ctx, w_ada, b_ada, g_mix_pre, g_mix_post, g_ffn_pre, g_ffn_post,
           w_in, conv_lru_w, conv_lru_b, w_rec_gate, b_rec_gate, w_in_gate, b_in_gate,
           lru_lambda, w_fourier, b_fourier, w_out, w_up, conv_ffn_w, conv_ffn_b, w_down):
    bsz, n, d = x.shape
    d_fourier = w_fourier.shape[1] * w_fourier.shape[2]
    d_lru = conv_lru_w.shape[2]
    assert w_ada.shape[0] == 1, "single layer: the context stream is only read, never updated"
    assert bsz + 1 <= MOD_ROWS and d_lru // N_LRU_HEADS == LANES
    l = 0
    pe = _grid_pos_tables(n, d)

    cond = jnp.zeros((MOD_ROWS, d), F32).at[:bsz].set(c).at[bsz].set(c_ctx)
    mod3 = _modulation(cond, w_ada[l], b_ada[l]).reshape(MOD_ROWS, N_MOD, d)

    w_in_b = w_in[l].astype(BF16)
    x_cols = (d_fourier, d_fourier + d_lru)
    (uxc,) = _in_proj(ctx, None, mod3, lambda b: bsz, g_mix_pre[l], w_in_b[:, x_cols[0]:x_cols[1]],
                      ((0, d_lru),), (F32,), ctx.shape[1], "in_proj_ctx")
    uf, ux, ug = _in_proj(x, pe, mod3, lambda b: b, g_mix_pre[l], w_in_b,
                          ((0, d_fourier), x_cols, (x_cols[1], x_cols[1] + d_lru)),
                          (BF16, F32, F32), IN_TM, "in_proj")

    wg = (0.5 * jnp.concatenate([w_rec_gate[l, 0], w_in_gate[l, 0], w_rec_gate[l, 1], w_in_gate[l, 1]],
                                axis=-1)).astype(BF16)
    bg = 0.5 * jnp.stack([b_rec_gate[l, 0], b_in_gate[l, 0], b_rec_gate[l, 1], b_in_gate[l, 1]])
    y_lru = _rg_lru(ux, uxc, ug, conv_lru_w[l], conv_lru_b[l], wg, bg, lru_lambda[l])
    y_fourier = _fourier(uf, w_fourier[l], b_fourier[l])

    x1, h2 = _out_proj(y_fourier, y_lru, x, pe, mod3, g_mix_post[l], g_ffn_pre[l], w_out[l].astype(BF16))
    return _conv_ffn(h2, x1, mod3, g_ffn_post[l], w_up[l].astype(BF16), conv_ffn_w[l], conv_ffn_b[l],
                     w_down[l].astype(BF16))
```

```python
import functools
import itertools
import math

import jax
import jax.numpy as jnp
from jax import lax
from jax.experimental import pallas as pl
from jax.experimental.pallas import tpu as pltpu

F32 = jnp.float32
BF16 = jnp.bfloat16

EPS = 1e-6
POS_BASE = 10000.0
GRID_W = 64
N_MOD = 6
N_FOURIER_GROUPS = 4
N_LRU_HEADS = 8
LRU_CONV = 4
LRU_C = 8.0
FFN_CONV = 3

LANES = 128
SUBLANES = 8
VMEM_BYTES = 64 * 1024 * 1024
VMEM_LIMIT_BYTES = 60000 * 1024
FFN_VMEM_LIMIT_BYTES = VMEM_BYTES - 2 * 1024 * 1024

MOD_ROWS = 16
IN_TM = 512
IN_SUB_TILES = 2
OUT_SUB_ROWS = (256, 256)
LRU_CB = 256
LRU_ROW_CHUNK = 512
SCAN_UNROLL = 4
FFN_TM = 1024
FFN_TF = 512
FFN_HALO = 16
FFN_EPILOGUE_ROWS = 256


def _cparams(sem, vmem_limit_bytes=VMEM_LIMIT_BYTES):
    return pltpu.CompilerParams(dimension_semantics=sem, vmem_limit_bytes=vmem_limit_bytes)


def _const_spec(shape):
    nd = len(shape)
    return pl.BlockSpec(shape, lambda *_: (0,) * nd, pipeline_mode=pl.Buffered(1))


def _mod_kernel(c_ref, w_ref, b_ref, o_ref):
    cv = c_ref[...]
    s = (cv * jax.nn.sigmoid(cv)).astype(BF16)
    o_ref[...] = jnp.dot(s, w_ref[...].astype(BF16), preferred_element_type=F32) + b_ref[...]


def _modulation(cond, w_ada, b_ada):
    d = cond.shape[1]
    n = w_ada.shape[1]
    tn = 1024
    return pl.pallas_call(
        _mod_kernel,
        grid=(n // tn,),
        in_specs=[
            pl.BlockSpec((MOD_ROWS, d), lambda j: (0, 0)),
            pl.BlockSpec((d, tn), lambda j: (0, j)),
            pl.BlockSpec((1, tn), lambda j: (0, j)),
        ],
        out_specs=pl.BlockSpec((MOD_ROWS, tn), lambda j: (0, j)),
        out_shape=jax.ShapeDtypeStruct((MOD_ROWS, n), F32),
        compiler_params=_cparams(("parallel",)),
        name="modulation",
    )(cond, w_ada, b_ada.reshape(1, n))


def _rms_norm(v, g):
    ms = jnp.mean(v * v, axis=-1, keepdims=True)
    return v * lax.rsqrt(ms + EPS) * g


def _tile_pos_embed(prow_ref, pcol_ref):
    rows = jnp.concatenate([jnp.broadcast_to(prow_ref[r:r + 1, :], pcol_ref.shape)
                            for r in range(prow_ref.shape[0])], axis=0)
    cols = jnp.concatenate([pcol_ref[...]] * prow_ref.shape[0], axis=0)
    return jnp.concatenate([rows, cols], axis=1)


def _in_proj_kernel(*refs, has_pe, col_splits):
    if has_pe:
        x_ref, prow_ref, pcol_ref, mod_ref, g_ref, w_ref = refs[:6]
        out_refs = refs[6:]
        pe = _tile_pos_embed(prow_ref, pcol_ref)
    else:
        x_ref, mod_ref, g_ref, w_ref = refs[:4]
        out_refs = refs[4:]
        pe = None
    n_rows = x_ref.shape[1]
    sub = n_rows // IN_SUB_TILES
    for r0 in range(0, n_rows, sub):
        rs = slice(r0, r0 + sub)
        xp = x_ref[0, rs, :] if pe is None else x_ref[0, rs, :] + pe[rs, :]
        h = _rms_norm(xp, g_ref[...]) * (1.0 + mod_ref[0, 1:2, :]) + mod_ref[0, 0:1, :]
        hb = h.astype(BF16)
        for o_ref, (c0, c1) in zip(out_refs, col_splits):
            o_ref[0, rs, :] = jnp.dot(hb, w_ref[:, c0:c1], preferred_element_type=F32).astype(o_ref.dtype)


def _pos_embed_specs(pe, tm):
    prow, pcol = pe
    assert tm % GRID_W == 0
    return [pl.BlockSpec((tm // GRID_W, prow.shape[1]), lambda i, b: (i, 0)), _const_spec(pcol.shape)]


def _in_proj(x, pe, mod3, mod_row, g, w, col_splits, out_dtypes, tm, name):
    bsz, n, d = x.shape
    has_pe = pe is not None
    in_specs = [pl.BlockSpec((1, tm, d), lambda i, b: (b, i, 0))]
    args = [x]
    if has_pe:
        in_specs += _pos_embed_specs(pe, tm)
        args += list(pe)
    in_specs += [
        pl.BlockSpec((1, N_MOD, d), lambda i, b: (mod_row(b), 0, 0)),
        _const_spec((1, d)),
        _const_spec(w.shape),
    ]
    args += [mod3, g.reshape(1, d), w]
    out_specs = [pl.BlockSpec((1, tm, c1 - c0), lambda i, b: (b, i, 0)) for c0, c1 in col_splits]
    out_shape = [jax.ShapeDtypeStruct((bsz, n, c1 - c0), dt) for (c0, c1), dt in zip(col_splits, out_dtypes)]
    return pl.pallas_call(
        functools.partial(_in_proj_kernel, has_pe=has_pe, col_splits=col_splits),
        grid=(n // tm, bsz),
        in_specs=in_specs,
        out_specs=out_specs,
        out_shape=out_shape,
        compiler_params=_cparams(("parallel", "parallel")),
        name=name,
    )(*args)


def _scan_pitch(n):
    m = -(-n // SUBLANES)
    while m % 8 != 4:
        m += 1
    return m


def _depthwise_conv_rows(x, w_ref, b_ref, left):
    n = x.shape[0]
    k_w = w_ref.shape[0]
    row = lax.broadcasted_iota(jnp.int32, x.shape, 0)
    y = b_ref[...] + jnp.zeros_like(x)
    for k in range(k_w):
        off = k - left
        if off == 0:
            xs = x
        elif off < 0:
            xs = jnp.where(row >= -off, pltpu.roll(x, -off, 0), 0.0)
        else:
            xs = jnp.where(row < n - off, pltpu.roll(x, n - off, 0), 0.0)
        y = y + xs * w_ref[k:k + 1, :]
    return y


def _log_sigmoid(z):
    return jnp.minimum(z, 0.0) - jnp.log1p(jnp.exp(-jnp.abs(z)))


def _lru_gates(xc_ref, n, row_chunk, wg_ref, bg_ref, quarter_c_ls, a_ref, b_ref, n_slabs):
    for r0 in range(0, n, row_chunk):
        for s in range(n_slabs):
            lanes = slice(s * LANES, (s + 1) * LANES)
            xh = xc_ref[r0:r0 + row_chunk, lanes]
            g = jnp.dot(xh.astype(BF16), wg_ref[s], preferred_element_type=F32)
            for d in range(2):
                tr = jnp.tanh(g[:, (2 * d) * LANES:(2 * d + 1) * LANES] + bg_ref[2 * d:2 * d + 1, lanes])
                ti = jnp.tanh(g[:, (2 * d + 1) * LANES:(2 * d + 2) * LANES] + bg_ref[2 * d + 1:2 * d + 2, lanes])
                hl = quarter_c_ls[d:d + 1, lanes]
                t = jnp.tanh(hl + hl * tr)
                q = pl.reciprocal(1.0 - t, full_range=False)
                nt = -t
                root = jnp.where(nt > 0.0, nt * lax.rsqrt(nt), 0.0)
                a_ref[d, s, r0:r0 + row_chunk, :] = (1.0 + t) * q
                b_ref[d, s, r0:r0 + row_chunk, :] = (root * q) * ((1.0 + ti) * xh)


def _lru_scan(a_ref, b_ref, sh_ref, sp_ref, n, h0, n_slabs, write_states):
    m = _scan_pitch(n)
    pairs = [(d, s) for d in range(2) for s in range(n_slabs)]
    sub = lax.broadcasted_iota(jnp.int32, (SUBLANES, LANES), 0)

    def row_of(d, j):
        return j if d == 0 else m - 1 - j

    def local_body(j, carry):
        out = []
        for (d, s), (h, p) in zip(pairs, carry):
            jj = row_of(d, j)
            a = a_ref[d, s, pl.ds(jj, SUBLANES, stride=m), :]
            b = b_ref[d, s, pl.ds(jj, SUBLANES, stride=m), :]
            h = a * h + b
            p = a * p
            if write_states:
                dst = pl.ds(pl.multiple_of(jj * SUBLANES, SUBLANES), SUBLANES)
                sh_ref[d, s, dst, :] = h
                sp_ref[d, s, dst, :] = p
            out.append((h, p))
        return tuple(out)

    init = tuple((jnp.zeros((SUBLANES, LANES), F32), jnp.ones((SUBLANES, LANES), F32)) for _ in pairs)
    local = lax.fori_loop(0, m, local_body, init, unroll=SCAN_UNROLL)

    carries, finals = [], []
    for (d, s), (hfin, ptot) in zip(pairs, local):
        first, last, shift = (0, SUBLANES - 1, 1) if d == 0 else (SUBLANES - 1, 0, SUBLANES - 1)
        hin = jnp.where(sub == first, jnp.broadcast_to(h0[d][s], (SUBLANES, LANES)), 0.0)
        for step in range(1, SUBLANES):
            k = first + step if d == 0 else first - step
            nxt = pltpu.roll(hfin + ptot * hin, shift, 0)
            hin = jnp.where(sub == k, nxt, hin)
        carries.append(hin)
        finals.append((hfin + ptot * hin)[last:last + 1, :])

    if write_states:
        def fix_body(j, _):
            for (d, s), hin in zip(pairs, carries):
                jj = row_of(d, j)
                src = pl.ds(pl.multiple_of(jj * SUBLANES, SUBLANES), SUBLANES)
                b_ref[d, s, pl.ds(jj, SUBLANES, stride=m), :] = sh_ref[d, s, src, :] + sp_ref[d, s, src, :] * hin
            return 0

        lax.fori_loop(0, m, fix_body, 0, unroll=SCAN_UNROLL)

    return [[finals[d * n_slabs + s] for s in range(n_slabs)] for d in range(2)]


def _lru_kernel(ux_ref, uxc_ref, ug_ref, cw_ref, cb_ref, wg_ref, bg_ref, lam_ref, o_ref,
                xc_s, xcc_s, a_s, b_s, ac_s, bc_s, sh_s, sp_s):
    n = ux_ref.shape[1]
    n_ctx = uxc_ref.shape[1]
    n_slabs = ux_ref.shape[2] // LANES
    ls = (0.25 * LRU_C) * _log_sigmoid(lam_ref[...])

    for ref, rows in ((a_s, n), (ac_s, n_ctx)):
        ref[:, :, rows:, :] = jnp.ones((2, n_slabs, ref.shape[2] - rows, LANES), F32)
    for ref, rows in ((b_s, n), (bc_s, n_ctx)):
        ref[:, :, rows:, :] = jnp.zeros((2, n_slabs, ref.shape[2] - rows, LANES), F32)

    left = LRU_CONV // 2
    xcc_s[...] = _depthwise_conv_rows(uxc_ref[0], cw_ref, cb_ref, left)
    xc_s[...] = _depthwise_conv_rows(ux_ref[0], cw_ref, cb_ref, left)

    _lru_gates(xcc_s, n_ctx, n_ctx, wg_ref, bg_ref, ls, ac_s, bc_s, n_slabs)
    _lru_gates(xc_s, n, LRU_ROW_CHUNK, wg_ref, bg_ref, ls, a_s, b_s, n_slabs)

    zero = jnp.zeros((1, LANES), F32)
    h0 = _lru_scan(ac_s, bc_s, None, None, n_ctx, [[zero] * n_slabs] * 2, n_slabs, False)
    _lru_scan(a_s, b_s, sh_s, sp_s, n, h0, n_slabs, True)

    for r0 in range(0, n, LRU_ROW_CHUNK):
        rows = slice(r0, r0 + LRU_ROW_CHUNK)
        for s in range(n_slabs):
            lanes = slice(s * LANES, (s + 1) * LANES)
            hsum = b_s[0, s, rows, :] + b_s[1, s, rows, :]
            gate = jax.nn.gelu(ug_ref[0, rows, lanes], approximate=True)
            o_ref[0, rows, lanes] = (hsum * gate).astype(o_ref.dtype)


def _rg_lru(ux, uxc, ug, conv_w, conv_b, wg, bg, lam):
    bsz, n, c = ux.shape
    n_ctx = uxc.shape[1]
    cb = LRU_CB
    n_slabs = cb // LANES
    lp = SUBLANES * _scan_pitch(n)
    cp = SUBLANES * _scan_pitch(n_ctx)
    seq = lambda rows: pl.BlockSpec((1, rows, cb), lambda b, j: (b, 0, j))
    par = lambda rows: pl.BlockSpec((rows, cb), lambda b, j: (0, j))
    return pl.pallas_call(
        _lru_kernel,
        grid=(bsz, c // cb),
        in_specs=[
            seq(n), seq(n_ctx), seq(n),
            par(conv_w.shape[0]), par(1),
            pl.BlockSpec((n_slabs, LANES, 4 * LANES), lambda b, j: (j, 0, 0)),
            par(4), par(2),
        ],
        out_specs=seq(n),
        out_shape=jax.ShapeDtypeStruct((bsz, n, c), BF16),
        scratch_shapes=[
            pltpu.VMEM((n, cb), F32),
            pltpu.VMEM((n_ctx, cb), F32),
            pltpu.VMEM((2, n_slabs, lp, LANES), F32),
            pltpu.VMEM((2, n_slabs, lp, LANES), F32),
            pltpu.VMEM((2, n_slabs, cp, LANES), F32),
            pltpu.VMEM((2, n_slabs, cp, LANES), F32),
            pltpu.VMEM((2, n_slabs, lp, LANES), F32),
            pltpu.VMEM((2, n_slabs, lp, LANES), F32),
        ],
        compiler_params=_cparams(("parallel", "parallel")),
        name="rg_lru",
    )(ux, uxc, ug, conv_w, conv_b.reshape(1, c), wg, bg, lam)


FFT_RADIX = 8
_FFT_BLOCK_OF = {0: (0, 0.0), 4: (4, 0.0), 2: (2, -1.0), 6: (2, 1.0), 1: (1, -1.0), 7: (1, 1.0), 3: (3, 1.0), 5: (3, -1.0)}


def _fourier_kernel(u_ref, m_ref, ccs_ref, wf_ref, bf_ref, o_ref, t_s, z_s, y_s):
    n = u_ref.shape[1]
    r = n // FFT_RADIX
    groups, gw, _ = wf_ref.shape
    x = [u_ref[0, p * r:(p + 1) * r, :].astype(F32) for p in range(FFT_RADIX)]
    e0, e1, e2, e3 = x[0] + x[4], x[0] - x[4], x[2] + x[6], x[2] - x[6]
    o0, o1, o2, o3 = x[1] + x[5], x[1] - x[5], x[3] + x[7], x[3] - x[7]
    ee, oo = e0 + e2, o0 + o2
    c = math.sqrt(0.5)
    p, m = c * (o1 - o3), c * (o1 + o3)
    t_re = {0: ee + oo, 4: ee - oo, 2: e0 - e2, 1: e1 + p, 3: e1 - p}
    t_im = {2: o0 - o2, 1: m + e3, 3: e3 - m}

    for blk in t_re:
        t_s[blk, 0] = t_re[blk].astype(BF16)
        if blk in t_im:
            t_s[blk, 1] = t_im[blk].astype(BF16)

    for k1 in range(FFT_RADIX):
        blk = _FFT_BLOCK_OF[k1][0]
        if blk in t_im:
            z = jnp.dot(m_ref[k1], t_s[blk].reshape(2 * r, -1), preferred_element_type=F32)
        else:
            z = jnp.dot(m_ref[k1, :, :r], t_s[blk, 0], preferred_element_type=F32)
        z_s[k1] = z.reshape(2, r, -1).astype(BF16)

    for g in range(groups):
        cols = slice(g * gw, (g + 1) * gw)
        f = jnp.dot(z_s[:, 0, :, cols].reshape(n, gw), ccs_ref[0], preferred_element_type=F32)
        f = f + jnp.dot(z_s[:, 1, :, cols].reshape(n, gw), ccs_ref[1], preferred_element_type=F32)
        y = jnp.dot(f.astype(BF16), wf_ref[g], preferred_element_type=F32) + bf_ref[:, cols]
        for k1 in range(FFT_RADIX):
            for h in range(gw // LANES):
                slab = g * (gw // LANES) + h
                y_s[slab, pl.ds(k1, r, stride=FFT_RADIX), :] = y[k1 * r:(k1 + 1) * r, h * LANES:(h + 1) * LANES]

    for slab in range(y_s.shape[0]):
        o_ref[0, :, slab * LANES:(slab + 1) * LANES] = y_s[slab].astype(o_ref.dtype)


def _fourier_tables(n, gw):
    r = n // FFT_RADIX
    k = jnp.arange(FFT_RADIX, dtype=jnp.int32)[:, None, None] + FFT_RADIX * jnp.arange(r, dtype=jnp.int32)[None, :, None]
    p2 = jnp.arange(r, dtype=jnp.int32)[None, None, :]
    ang = ((k * p2) % n).astype(F32) * (2.0 * math.pi / n)
    cn, sn = jnp.cos(ang) * n ** -0.5, jnp.sin(ang) * n ** -0.5
    sign = jnp.array([_FFT_BLOCK_OF[k1][1] for k1 in range(FFT_RADIX)], F32)[:, None, None]
    m = jnp.concatenate([jnp.concatenate([cn, sign * sn], axis=2),
                         jnp.concatenate([-sn, sign * cn], axis=2)], axis=1)
    idx = jnp.arange(gw, dtype=jnp.int32)
    ang_c = ((idx[:, None] * idx[None, :]) % gw).astype(F32) * (2.0 * math.pi / gw)
    ccs = jnp.stack([jnp.cos(ang_c), jnp.sin(ang_c)]) * gw ** -0.5
    return m.astype(BF16), ccs.astype(BF16)


def _fourier(uf, w_f, b_f):
    bsz, n, c = uf.shape
    groups, gw, _ = w_f.shape
    assert n % FFT_RADIX == 0 and gw % LANES == 0
    m, ccs = _fourier_tables(n, gw)
    return pl.pallas_call(
        _fourier_kernel,
        grid=(bsz,),
        in_specs=[
            pl.BlockSpec((1, n, c), lambda b: (b, 0, 0)),
            _const_spec(m.shape),
            _const_spec(ccs.shape),
            _const_spec(w_f.shape),
            _const_spec((1, c)),
        ],
        out_specs=pl.BlockSpec((1, n, c), lambda b: (b, 0, 0)),
        out_shape=jax.ShapeDtypeStruct((bsz, n, c), BF16),
        scratch_shapes=[
            pltpu.VMEM((FFT_RADIX // 2 + 1, 2, n // FFT_RADIX, c), BF16),
            pltpu.VMEM((FFT_RADIX, 2, n // FFT_RADIX, c), BF16),
            pltpu.VMEM((c // LANES, n, LANES), F32),
        ],
        compiler_params=_cparams(("parallel",)),
        name="fourier",
    )(uf, m, ccs, w_f.astype(BF16), b_f.reshape(1, c))


def _out_proj_kernel(yf_ref, yl_ref, x_ref, prow_ref, pcol_ref, mod_ref, gpost_ref, gpre_ref, w_ref,
                     x1_ref, h2_ref):
    k = yf_ref.shape[2]
    pe = _tile_pos_embed(prow_ref, pcol_ref)
    assert sum(OUT_SUB_ROWS) == x_ref.shape[1]
    for r0, n_rows in zip(itertools.accumulate((0,) + OUT_SUB_ROWS), OUT_SUB_ROWS):
        rs = slice(r0, r0 + n_rows)
        y = jnp.dot(yf_ref[0, rs, :], w_ref[:k, :], preferred_element_type=F32)
        y = y + jnp.dot(yl_ref[0, rs, :], w_ref[k:, :], preferred_element_type=F32)
        x1 = (x_ref[0, rs, :] + pe[rs, :]) + mod_ref[0, 2:3, :] * _rms_norm(y, gpost_ref[...])
        x1_ref[0, rs, :] = x1
        h2 = _rms_norm(x1, gpre_ref[...]) * (1.0 + mod_ref[0, 4:5, :]) + mod_ref[0, 3:4, :]
        h2_ref[0, rs, :] = h2.astype(h2_ref.dtype)


def _out_proj(yf, yl, x, pe, mod3, g_post, g_pre, w_out):
    bsz, n, d = x.shape
    k = yf.shape[2]
    tm = IN_TM
    act = pl.BlockSpec((1, tm, k), lambda i, b: (b, i, 0))
    row = pl.BlockSpec((1, tm, d), lambda i, b: (b, i, 0))
    return pl.pallas_call(
        _out_proj_kernel,
        grid=(n // tm, bsz),
        in_specs=[
            act, act, row,
            *_pos_embed_specs(pe, tm),
            pl.BlockSpec((1, N_MOD, d), lambda i, b: (b, 0, 0)),
            _const_spec((1, d)), _const_spec((1, d)),
            _const_spec(w_out.shape),
        ],
        out_specs=[row, row],
        out_shape=[jax.ShapeDtypeStruct((bsz, n, d), F32), jax.ShapeDtypeStruct((bsz, n, d), BF16)],
        compiler_params=_cparams(("parallel", "parallel")),
        name="out_proj",
    )(yf, yl, x, *pe, mod3, g_post.reshape(1, d), g_pre.reshape(1, d), w_out)


def _ffn_kernel(h_hbm, x1_hbm, wg_ref, wv_ref, cwg_ref, cwv_ref, cbg_ref, cbv_ref, wd_ref, mod_ref, gpost_ref,
                o_hbm, lhs_s, acc_s, xo_s, lhs_sem, x1_sem, out_sem, *, n_t, n_f):
    b, i, c = pl.program_id(0), pl.program_id(1), pl.program_id(2)
    tm = acc_s.shape[0]
    rows = lhs_s.shape[1]
    halo = (rows - tm) // 2
    t = b * n_t + i
    slot = t % 2
    last_tile = pl.num_programs(0) * n_t - 1

    def lhs_dma(bb, ii, sl, start):
        def run(src_row0, dst_row0, n_rows):
            cp = pltpu.make_async_copy(h_hbm.at[bb, pl.ds(src_row0, n_rows)],
                                       lhs_s.at[sl, pl.ds(dst_row0, n_rows)], lhs_sem.at[sl])
            if start:
                cp.start()
            else:
                cp.wait()

        pl.when(ii == 0)(lambda: run(0, halo, tm + halo))
        pl.when(ii == n_t - 1)(lambda: run((n_t - 1) * tm - halo, 0, tm + halo))
        if n_t > 2:
            pl.when(jnp.logical_and(ii > 0, ii < n_t - 1))(
                lambda: run(pl.multiple_of(ii * tm - halo, halo), 0, rows))

    def x1_copy():
        return pltpu.make_async_copy(x1_hbm.at[b, pl.ds(pl.multiple_of(i * tm, tm), tm)], xo_s, x1_sem.at[0])

    def out_copy(bb, ii):
        return pltpu.make_async_copy(xo_s, o_hbm.at[bb, pl.ds(pl.multiple_of(ii * tm, tm), tm)], out_sem.at[0])

    def conv_branch(w_ref, cw_ref, cb_ref):
        up = jnp.dot(lhs_s[slot], w_ref[...], preferred_element_type=F32)
        left = FFN_CONV // 2
        y = cb_ref[...] + jnp.zeros((tm, up.shape[1]), F32)
        for k in range(FFN_CONV):
            off = k - left
            shifted = up if off == 0 else pltpu.roll(up, (-off) % rows, 0)
            y = y + shifted[halo:halo + tm, :] * cw_ref[k:k + 1, :]
        return y

    @pl.when(c == 0)
    def _():
        pl.when(t == 0)(lambda: lhs_dma(b, i, slot, True))
        lhs_dma(b, i, slot, False)
        zeros = jnp.zeros((halo, lhs_s.shape[2]), lhs_s.dtype)

        @pl.when(i == 0)
        def _():
            lhs_s[slot, 0:halo, :] = zeros

        @pl.when(i == n_t - 1)
        def _():
            lhs_s[slot, halo + tm:, :] = zeros

        acc_s[...] = jnp.zeros_like(acc_s)

    @pl.when(c == 1)
    def _():
        nxt = i + 1 < n_t
        pl.when(t < last_tile)(
            lambda: lhs_dma(jnp.where(nxt, b, b + 1), jnp.where(nxt, i + 1, 0), 1 - slot, True))
        prv = i > 0
        pl.when(t > 0)(lambda: out_copy(jnp.where(prv, b, b - 1), jnp.where(prv, i - 1, n_t - 1)).wait())
        x1_copy().start()

    g = conv_branch(wg_ref, cwg_ref, cbg_ref)
    v = conv_branch(wv_ref, cwv_ref, cbv_ref)
    act = (jax.nn.gelu(g, approximate=True) * v).astype(BF16)
    acc_s[...] += jnp.dot(act, wd_ref[...], preferred_element_type=F32)

    @pl.when(c == n_f - 1)
    def _():
        x1_copy().wait()
        for r0 in range(0, tm, FFN_EPILOGUE_ROWS):
            rs = slice(r0, r0 + FFN_EPILOGUE_ROWS)
            xo_s[rs, :] = xo_s[rs, :] + mod_ref[0, 5:6, :] * _rms_norm(acc_s[rs, :], gpost_ref[...])
        out_copy(b, i).start()
        pl.when(t == last_tile)(lambda: out_copy(b, i).wait())


def _conv_ffn(h2, x1, mod3, g_post, w_up, conv_w, conv_b, w_down):
    bsz, n, d = h2.shape
    d_ff = w_down.shape[0]
    tm, tf, halo = FFN_TM, FFN_TF, FFN_HALO
    nt, nf = n // tm, d_ff // tf
    assert nt >= 2 and n % tm == 0 and d_ff % tf == 0 and tm % FFN_EPILOGUE_ROWS == 0
    gcol = lambda r: pl.BlockSpec((r, tf), lambda b, i, c: (0, c))
    vcol = lambda r: pl.BlockSpec((r, tf), lambda b, i, c: (0, c + nf))
    any_spec = pl.BlockSpec(memory_space=pl.ANY)
    conv_b2 = conv_b.reshape(1, 2 * d_ff)
    rows = tm + 2 * halo
    return pl.pallas_call(
        functools.partial(_ffn_kernel, n_t=nt, n_f=nf),
        grid=(bsz, nt, nf),
        in_specs=[
            any_spec, any_spec,
            gcol(d), vcol(d),
            gcol(FFN_CONV), vcol(FFN_CONV),
            gcol(1), vcol(1),
            pl.BlockSpec((tf, d), lambda b, i, c: (c, 0)),
            pl.BlockSpec((1, N_MOD, d), lambda b, i, c: (b, 0, 0)),
            _const_spec((1, d)),
        ],
        out_specs=any_spec,
        out_shape=jax.ShapeDtypeStruct((bsz, n, d), F32),
        scratch_shapes=[
            pltpu.VMEM((2, rows, d), BF16),
            pltpu.VMEM((tm, d), F32),
            pltpu.VMEM((tm, d), F32),
            pltpu.SemaphoreType.DMA((2,)),
            pltpu.SemaphoreType.DMA((1,)),
            pltpu.SemaphoreType.DMA((1,)),
        ],
        compiler_params=_cparams(("arbitrary", "arbitrary", "arbitrary"), FFN_VMEM_LIMIT_BYTES),
        name="conv_ffn",
    )(h2, x1, w_up, w_up, conv_w, conv_w, conv_b2, conv_b2, w_down, mod3, g_post.reshape(1, d))


def _grid_pos_tables(n_tokens, d_model):
    quarter = d_model // 4
    freqs = POS_BASE ** (-jnp.arange(quarter, dtype=F32) / quarter)

    def enc(count):
        ang = jnp.arange(count, dtype=F32)[:, None] * freqs[None, :]
        return jnp.concatenate([jnp.sin(ang), jnp.cos(ang)], axis=-1)

    return enc(n_tokens // GRID_W), enc(GRID_W)


def kernel(x, c, ctx, c_ctx, w_ada, b_ada, g_mix_pre, g_mix_post, g_ffn_pre, g_ffn_post,
           w_in, conv_lru_w, conv_lru_b, w_rec_gate, b_rec_gate, w_in_gate, b_in_gate,
           lru_lambda, w_fourier, b_fourier, w_out, w_up, conv_ffn_w, conv_ffn_b, w_down):
    bsz, n, d = x.shape
    d_fourier = w_fourier.shape[1] * w_fourier.shape[2]
    d_lru = conv_lru_w.shape[2]
    assert w_ada.shape[0] == 1, "single layer: the context stream is only read, never updated"
    assert bsz + 1 <= MOD_ROWS and d_lru // N_LRU_HEADS == LANES
    l = 0
    pe = _grid_pos_tables(n, d)

    cond = jnp.zeros((MOD_ROWS, d), F32).at[:bsz].set(c).at[bsz].set(c_ctx)
    mod3 = _modulation(cond, w_ada[l], b_ada[l]).reshape(MOD_ROWS, N_MOD, d)

    w_in_b = w_in[l].astype(BF16)
    x_cols = (d_fourier, d_fourier + d_lru)
    (uxc,) = _in_proj(ctx, None, mod3, lambda b: bsz, g_mix_pre[l], w_in_b[:, x_cols[0]:x_cols[1]],
                      ((0, d_lru),), (F32,), ctx.shape[1], "in_proj_ctx")
    uf, ux, ug = _in_proj(x, pe, mod3, lambda b: b, g_mix_pre[l], w_in_b,
                          ((0, d_fourier), x_cols, (x_cols[1], x_cols[1] + d_lru)),
                          (BF16, F32, F32), IN_TM, "in_proj")

    wg = (0.5 * jnp.concatenate([w_rec_gate[l, 0], w_in_gate[l, 0], w_rec_gate[l, 1], w_in_gate[l, 1]],
                                axis=-1)).astype(BF16)
    bg = 0.5 * jnp.stack([b_rec_gate[l, 0], b_in_gate[l, 0], b_rec_gate[l, 1], b_in_gate[l, 1]])
    y_lru = _rg_lru(ux, uxc, ug, conv_lru_w[l], conv_lru_b[l], wg, bg, lru_lambda[l])
    y_fourier = _fourier(uf, w_fourier[l], b_fourier[l])

    x1, h2 = _out_proj(y_fourier, y_lru, x, pe, mod3, g_mix_post[l], g_ffn_pre[l], w_out[l].astype(BF16))
    return _conv_ffn(h2, x1, mod3, g_ffn_post[l], w_up[l].astype(BF16), conv_ffn_w[l], conv_ffn_b[l],
                     w_down[l].astype(BF16))
```

```python
import functools
import itertools
import math

import jax
import jax.numpy as jnp
from jax import lax
from jax.experimental import pallas as pl
from jax.experimental.pallas import tpu as pltpu

F32 = jnp.float32
BF16 = jnp.bfloat16

EPS = 1e-6
POS_BASE = 10000.0
GRID_W = 64
N_MOD = 6
N_FOURIER_GROUPS = 4
N_LRU_HEADS = 8
LRU_CONV = 4
LRU_C = 8.0
FFN_CONV = 3

LANES = 128
SUBLANES = 8
VMEM_BYTES = 64 * 1024 * 1024
VMEM_LIMIT_BYTES = 60000 * 1024
FFN_VMEM_LIMIT_BYTES = VMEM_BYTES - 2 * 1024 * 1024

MOD_ROWS = 16
IN_TM = 512
IN_SUB_TILES = 2
OUT_SUB_ROWS = (256, 256)
LRU_CB = 256
LRU_ROW_CHUNK = 512
SCAN_UNROLL = 4
FFN_TM = 1024
FFN_TF = 512
FFN_HALO = 16
FFN_EPILOGUE_ROWS = 256


def _cparams(sem, vmem_limit_bytes=VMEM_LIMIT_BYTES):
    return pltpu.CompilerParams(dimension_semantics=sem, vmem_limit_bytes=vmem_limit_bytes)


def _const_spec(shape):
    nd = len(shape)
    return pl.BlockSpec(shape, lambda *_: (0,) * nd, pipeline_mode=pl.Buffered(1))


def _mod_kernel(c_ref, w_ref, b_ref, o_ref):
    cv = c_ref[...]
    s = (cv * jax.nn.sigmoid(cv)).astype(BF16)
    o_ref[...] = jnp.dot(s, w_ref[...].astype(BF16), preferred_element_type=F32) + b_ref[...]


def _modulation(cond, w_ada, b_ada):
    d = cond.shape[1]
    n = w_ada.shape[1]
    tn = 1024
    return pl.pallas_call(
        _mod_kernel,
        grid=(n // tn,),
        in_specs=[
            pl.BlockSpec((MOD_ROWS, d), lambda j: (0, 0)),
            pl.BlockSpec((d, tn), lambda j: (0, j)),
            pl.BlockSpec((1, tn), lambda j: (0, j)),
        ],
        out_specs=pl.BlockSpec((MOD_ROWS, tn), lambda j: (0, j)),
        out_shape=jax.ShapeDtypeStruct((MOD_ROWS, n), F32),
        compiler_params=_cparams(("parallel",)),
        name="modulation",
    )(cond, w_ada, b_ada.reshape(1, n))


def _rms_norm(v, g):
    ms = jnp.mean(v * v, axis=-1, keepdims=True)
    return v * lax.rsqrt(ms + EPS) * g


def _tile_pos_embed(prow_ref, pcol_ref):
    rows = jnp.concatenate([jnp.broadcast_to(prow_ref[r:r + 1, :], pcol_ref.shape)
                            for r in range(prow_ref.shape[0])], axis=0)
    cols = jnp.concatenate([pcol_ref[...]] * prow_ref.shape[0], axis=0)
    return jnp.concatenate([rows, cols], axis=1)


def _in_proj_kernel(*refs, has_pe, col_splits):
    if has_pe:
        x_ref, prow_ref, pcol_ref, mod_ref, g_ref, w_ref = refs[:6]
        out_refs = refs[6:]
        pe = _tile_pos_embed(prow_ref, pcol_ref)
    else:
        x_ref, mod_ref, g_ref, w_ref = refs[:4]
        out_refs = refs[4:]
        pe = None
    n_rows = x_ref.shape[1]
    sub = n_rows // IN_SUB_TILES
    for r0 in range(0, n_rows, sub):
        rs = slice(r0, r0 + sub)
        xp = x_ref[0, rs, :] if pe is None else x_ref[0, rs, :] + pe[rs, :]
        h = _rms_norm(xp, g_ref[...]) * (1.0 + mod_ref[0, 1:2, :]) + mod_ref[0, 0:1, :]
        hb = h.astype(BF16)
        for o_ref, (c0, c1) in zip(out_refs, col_splits):
            o_ref[0, rs, :] = jnp.dot(hb, w_ref[:, c0:c1], preferred_element_type=F32).astype(o_ref.dtype)


def _pos_embed_specs(pe, tm):
    prow, pcol = pe
    assert tm % GRID_W == 0
    return [pl.BlockSpec((tm // GRID_W, prow.shape[1]), lambda i, b: (i, 0)), _const_spec(pcol.shape)]


def _in_proj(x, pe, mod3, mod_row, g, w, col_splits, out_dtypes, tm, name):
    bsz, n, d = x.shape
    has_pe = pe is not None
    in_specs = [pl.BlockSpec((1, tm, d), lambda i, b: (b, i, 0))]
    args = [x]
    if has_pe:
        in_specs += _pos_embed_specs(pe, tm)
        args += list(pe)
    in_specs += [
        pl.BlockSpec((1, N_MOD, d), lambda i, b: (mod_row(b), 0, 0)),
        _const_spec((1, d)),
        _const_spec(w.shape),
    ]
    args += [mod3, g.reshape(1, d), w]
    out_specs = [pl.BlockSpec((1, tm, c1 - c0), lambda i, b: (b, i, 0)) for c0, c1 in col_splits]
    out_shape = [jax.ShapeDtypeStruct((bsz, n, c1 - c0), dt) for (c0, c1), dt in zip(col_splits, out_dtypes)]
    return pl.pallas_call(
        functools.partial(_in_proj_kernel, has_pe=has_pe, col_splits=col_splits),
        grid=(n // tm, bsz),
        in_specs=in_specs,
        out_specs=out_specs,
        out_shape=out_shape,
        compiler_params=_cparams(("parallel", "parallel")),
        name=name,
    )(*args)


def _scan_pitch(n):
    m = -(-n // SUBLANES)
    while m % 8 != 4:
        m += 1
    return m


def _depthwise_conv_rows(x, w_ref, b_ref, left):
    n = x.shape[0]
    k_w = w_ref.shape[0]
    row = lax.broadcasted_iota(jnp.int32, x.shape, 0)
    y = b_ref[...] + jnp.zeros_like(x)
    for k in range(k_w):
        off = k - left
        if off == 0:
            xs = x
        elif off < 0:
            xs = jnp.where(row >= -off, pltpu.roll(x, -off, 0), 0.0)
        else:
            xs = jnp.where(row < n - off, pltpu.roll(x, n - off, 0), 0.0)
        y = y + xs * w_ref[k:k + 1, :]
    return y


def _log_sigmoid(z):
    return jnp.minimum(z, 0.0) - jnp.log1p(jnp.exp(-jnp.abs(z)))


def _lru_gates(xc_ref, n, row_chunk, wg_ref, bg_ref, quarter_c_ls, a_ref, b_ref, n_slabs):
    for r0 in range(0, n, row_chunk):
        for s in range(n_slabs):
            lanes = slice(s * LANES, (s + 1) * LANES)
            xh = xc_ref[r0:r0 + row_chunk, lanes]
            g = jnp.dot(xh.astype(BF16), wg_ref[s], preferred_element_type=F32)
            for d in range(2):
                tr = jnp.tanh(g[:, (2 * d) * LANES:(2 * d + 1) * LANES] + bg_ref[2 * d:2 * d + 1, lanes])
                ti = jnp.tanh(g[:, (2 * d + 1) * LANES:(2 * d + 2) * LANES] + bg_ref[2 * d + 1:2 * d + 2, lanes])
                hl = quarter_c_ls[d:d + 1, lanes]
                t = jnp.tanh(hl + hl * tr)
                q = pl.reciprocal(1.0 - t, full_range=False)
                nt = -t
                root = jnp.where(nt > 0.0, nt * lax.rsqrt(nt), 0.0)
                a_ref[d, s, r0:r0 + row_chunk, :] = (1.0 + t) * q
                b_ref[d, s, r0:r0 + row_chunk, :] = (root * q) * ((1.0 + ti) * xh)


def _lru_scan(a_ref, b_ref, sh_ref, sp_ref, n, h0, n_slabs, write_states):
    m = _scan_pitch(n)
    pairs = [(d, s) for d in range(2) for s in range(n_slabs)]
    sub = lax.broadcasted_iota(jnp.int32, (SUBLANES, LANES), 0)

    def row_of(d, j):
        return j if d == 0 else m - 1 - j

    def local_body(j, carry):
        out = []
        for (d, s), (h, p) in zip(pairs, carry):
            jj = row_of(d, j)
            a = a_ref[d, s, pl.ds(jj, SUBLANES, stride=m), :]
            b = b_ref[d, s, pl.ds(jj, SUBLANES, stride=m), :]
            h = a * h + b
            p = a * p
            if write_states:
                dst = pl.ds(pl.multiple_of(jj * SUBLANES, SUBLANES), SUBLANES)
                sh_ref[d, s, dst, :] = h
                sp_ref[d, s, dst, :] = p
            out.append((h, p))
        return tuple(out)

    init = tuple((jnp.zeros((SUBLANES, LANES), F32), jnp.ones((SUBLANES, LANES), F32)) for _ in pairs)
    local = lax.fori_loop(0, m, local_body, init, unroll=SCAN_UNROLL)

    carries, finals = [], []
    for (d, s), (hfin, ptot) in zip(pairs, local):
        first, last, shift = (0, SUBLANES - 1, 1) if d == 0 else (SUBLANES - 1, 0, SUBLANES - 1)
        hin = jnp.where(sub == first, jnp.broadcast_to(h0[d][s], (SUBLANES, LANES)), 0.0)
        for step in range(1, SUBLANES):
            k = first + step if d == 0 else first - step
            nxt = pltpu.roll(hfin + ptot * hin, shift, 0)
            hin = jnp.where(sub == k, nxt, hin)
        carries.append(hin)
        finals.append((hfin + ptot * hin)[last:last + 1, :])

    if write_states:
        def fix_body(j, _):
            for (d, s), hin in zip(pairs, carries):
                jj = row_of(d, j)
                src = pl.ds(pl.multiple_of(jj * SUBLANES, SUBLANES), SUBLANES)
                b_ref[d, s, pl.ds(jj, SUBLANES, stride=m), :] = sh_ref[d, s, src, :] + sp_ref[d, s, src, :] * hin
            return 0

        lax.fori_loop(0, m, fix_body, 0, unroll=SCAN_UNROLL)

    return [[finals[d * n_slabs + s] for s in range(n_slabs)] for d in range(2)]


def _lru_kernel(ux_ref, uxc_ref, ug_ref, cw_ref, cb_ref, wg_ref, bg_ref, lam_ref, o_ref,
                xc_s, xcc_s, a_s, b_s, ac_s, bc_s, sh_s, sp_s):
    n = ux_ref.shape[1]
    n_ctx = uxc_ref.shape[1]
    n_slabs = ux_ref.shape[2] // LANES
    ls = (0.25 * LRU_C) * _log_sigmoid(lam_ref[...])

    for ref, rows in ((a_s, n), (ac_s, n_ctx)):
        ref[:, :, rows:, :] = jnp.ones((2, n_slabs, ref.shape[2] - rows, LANES), F32)
    for ref, rows in ((b_s, n), (bc_s, n_ctx)):
        ref[:, :, rows:, :] = jnp.zeros((2, n_slabs, ref.shape[2] - rows, LANES), F32)

    left = LRU_CONV // 2
    xcc_s[...] = _depthwise_conv_rows(uxc_ref[0], cw_ref, cb_ref, left)
    xc_s[...] = _depthwise_conv_rows(ux_ref[0], cw_ref, cb_ref, left)

    _lru_gates(xcc_s, n_ctx, n_ctx, wg_ref, bg_ref, ls, ac_s, bc_s, n_slabs)
    _lru_gates(xc_s, n, LRU_ROW_CHUNK, wg_ref, bg_ref, ls, a_s, b_s, n_slabs)

    zero = jnp.zeros((1, LANES), F32)
    h0 = _lru_scan(ac_s, bc_s, None, None, n_ctx, [[zero] * n_slabs] * 2, n_slabs, False)
    _lru_scan(a_s, b_s, sh_s, sp_s, n, h0, n_slabs, True)

    for r0 in range(0, n, LRU_ROW_CHUNK):
        rows = slice(r0, r0 + LRU_ROW_CHUNK)
        for s in range(n_slabs):
            lanes = slice(s * LANES, (s + 1) * LANES)
            hsum = b_s[0, s, rows, :] + b_s[1, s, rows, :]
            gate = jax.nn.gelu(ug_ref[0, rows, lanes], approximate=True)
            o_ref[0, rows, lanes] = (hsum * gate).astype(o_ref.dtype)


def _rg_lru(ux, uxc, ug, conv_w, conv_b, wg, bg, lam):
    bsz, n, c = ux.shape
    n_ctx = uxc.shape[1]
    cb = LRU_CB
    n_slabs = cb // LANES
    lp = SUBLANES * _scan_pitch(n)
    cp = SUBLANES * _scan_pitch(n_ctx)
    seq = lambda rows: pl.BlockSpec((1, rows, cb), lambda b, j: (b, 0, j))
    par = lambda rows: pl.BlockSpec((rows, cb), lambda b, j: (0, j))
    return pl.pallas_call(
        _lru_kernel,
        grid=(bsz, c // cb),
        in_specs=[
            seq(n), seq(n_ctx), seq(n),
            par(conv_w.shape[0]), par(1),
            pl.BlockSpec((n_slabs, LANES, 4 * LANES), lambda b, j: (j, 0, 0)),
            par(4), par(2),
        ],
        out_specs=seq(n),
        out_shape=jax.ShapeDtypeStruct((bsz, n, c), BF16),
        scratch_shapes=[
            pltpu.VMEM((n, cb), F32),
            pltpu.VMEM((n_ctx, cb), F32),
            pltpu.VMEM((2, n_slabs, lp, LANES), F32),
            pltpu.VMEM((2, n_slabs, lp, LANES), F32),
            pltpu.VMEM((2, n_slabs, cp, LANES), F32),
            pltpu.VMEM((2, n_slabs, cp, LANES), F32),
            pltpu.VMEM((2, n_slabs, lp, LANES), F32),
            pltpu.VMEM((2, n_slabs, lp, LANES), F32),
        ],
        compiler_params=_cparams(("parallel", "parallel")),
        name="rg_lru",
    )(ux, uxc, ug, conv_w, conv_b.reshape(1, c), wg, bg, lam)


FFT_RADIX = 8
_FFT_BLOCK_OF = {0: (0, 0.0), 4: (4, 0.0), 2: (2, -1.0), 6: (2, 1.0), 1: (1, -1.0), 7: (1, 1.0), 3: (3, 1.0), 5: (3, -1.0)}


def _fourier_kernel(u_ref, m_ref, ccs_ref, wf_ref, bf_ref, o_ref, t_s, z_s, y_s):
    n = u_ref.shape[1]
    r = n // FFT_RADIX
    groups, gw, _ = wf_ref.shape
    x = [u_ref[0, p * r:(p + 1) * r, :].astype(F32) for p in range(FFT_RADIX)]
    e0, e1, e2, e3 = x[0] + x[4], x[0] - x[4], x[2] + x[6], x[2] - x[6]
    o0, o1, o2, o3 = x[1] + x[5], x[1] - x[5], x[3] + x[7], x[3] - x[7]
    ee, oo = e0 + e2, o0 + o2
    c = math.sqrt(0.5)
    p, m = c * (o1 - o3), c * (o1 + o3)
    t_re = {0: ee + oo, 4: ee - oo, 2: e0 - e2, 1: e1 + p, 3: e1 - p}
    t_im = {2: o0 - o2, 1: m + e3, 3: e3 - m}

    for blk in t_re:
        t_s[blk, 0] = t_re[blk].astype(BF16)
        if blk in t_im:
            t_s[blk, 1] = t_im[blk].astype(BF16)

    for k1 in range(FFT_RADIX):
        blk = _FFT_BLOCK_OF[k1][0]
        if blk in t_im:
            z = jnp.dot(m_ref[k1], t_s[blk].reshape(2 * r, -1), preferred_element_type=F32)
        else:
            z = jnp.dot(m_ref[k1, :, :r], t_s[blk, 0], preferred_element_type=F32)
        z_s[k1] = z.reshape(2, r, -1).astype(BF16)

    for g in range(groups):
        cols = slice(g * gw, (g + 1) * gw)
        f = jnp.dot(z_s[:, 0, :, cols].reshape(n, gw), ccs_ref[0], preferred_element_type=F32)
        f = f + jnp.dot(z_s[:, 1, :, cols].reshape(n, gw), ccs_ref[1], preferred_element_type=F32)
        y = jnp.dot(f.astype(BF16), wf_ref[g], preferred_element_type=F32) + bf_ref[:, cols]
        for k1 in range(FFT_RADIX):
            for h in range(gw // LANES):
                slab = g * (gw // LANES) + h
                y_s[slab, pl.ds(k1, r, stride=FFT_RADIX), :] = y[k1 * r:(k1 + 1) * r, h * LANES:(h + 1) * LANES]

    for slab in range(y_s.shape[0]):
        o_ref[0, :, slab * LANES:(slab + 1) * LANES] = y_s[slab].astype(o_ref.dtype)


def _fourier_tables(n, gw):
    r = n // FFT_RADIX
    k = jnp.arange(FFT_RADIX, dtype=jnp.int32)[:, None, None] + FFT_RADIX * jnp.arange(r, dtype=jnp.int32)[None, :, None]
    p2 = jnp.arange(r, dtype=jnp.int32)[None, None, :]
    ang = ((k * p2) % n).astype(F32) * (2.0 * math.pi / n)
    cn, sn = jnp.cos(ang) * n ** -0.5, jnp.sin(ang) * n ** -0.5
    sign = jnp.array([_FFT_BLOCK_OF[k1][1] for k1 in range(FFT_RADIX)], F32)[:, None, None]
    m = jnp.concatenate([jnp.concatenate([cn, sign * sn], axis=2),
                         jnp.concatenate([-sn, sign * cn], axis=2)], axis=1)
    idx = jnp.arange(gw, dtype=jnp.int32)
    ang_c = ((idx[:, None] * idx[None, :]) % gw).astype(F32) * (2.0 * math.pi / gw)
    ccs = jnp.stack([jnp.cos(ang_c), jnp.sin(ang_c)]) * gw ** -0.5
    return m.astype(BF16), ccs.astype(BF16)


def _fourier(uf, w_f, b_f):
    bsz, n, c = uf.shape
    groups, gw, _ = w_f.shape
    assert n % FFT_RADIX == 0 and gw % LANES == 0
    m, ccs = _fourier_tables(n, gw)
    return pl.pallas_call(
        _fourier_kernel,
        grid=(bsz,),
        in_specs=[
            pl.BlockSpec((1, n, c), lambda b: (b, 0, 0)),
            _const_spec(m.shape),
            _const_spec(ccs.shape),
            _const_spec(w_f.shape),
            _const_spec((1, c)),
        ],
        out_specs=pl.BlockSpec((1, n, c), lambda b: (b, 0, 0)),
        out_shape=jax.ShapeDtypeStruct((bsz, n, c), BF16),
        scratch_shapes=[
            pltpu.VMEM((FFT_RADIX // 2 + 1, 2, n // FFT_RADIX, c), BF16),
            pltpu.VMEM((FFT_RADIX, 2, n // FFT_RADIX, c), BF16),
            pltpu.VMEM((c // LANES, n, LANES), F32),
        ],
        compiler_params=_cparams(("parallel",)),
        name="fourier",
    )(uf, m, ccs, w_f.astype(BF16), b_f.reshape(1, c))


def _out_proj_kernel(yf_ref, yl_ref, x_ref, prow_ref, pcol_ref, mod_ref, gpost_ref, gpre_ref, w_ref,
                     x1_ref, h2_ref):
    k = yf_ref.shape[2]
    pe = _tile_pos_embed(prow_ref, pcol_ref)
    assert sum(OUT_SUB_ROWS) == x_ref.shape[1]
    for r0, n_rows in zip(itertools.accumulate((0,) + OUT_SUB_ROWS), OUT_SUB_ROWS):
        rs = slice(r0, r0 + n_rows)
        y = jnp.dot(yf_ref[0, rs, :], w_ref[:k, :], preferred_element_type=F32)
        y = y + jnp.dot(yl_ref[0, rs, :], w_ref[k:, :], preferred_element_type=F32)
        x1 = (x_ref[0, rs, :] + pe[rs, :]) + mod_ref[0, 2:3, :] * _rms_norm(y, gpost_ref[...])
        x1_ref[0, rs, :] = x1
        h2 = _rms_norm(x1, gpre_ref[...]) * (1.0 + mod_ref[0, 4:5, :]) + mod_ref[0, 3:4, :]
        h2_ref[0, rs, :] = h2.astype(h2_ref.dtype)


def _out_proj(yf, yl, x, pe, mod3, g_post, g_pre, w_out):
    bsz, n, d = x.shape
    k = yf.shape[2]
    tm = IN_TM
    act = pl.BlockSpec((1, tm, k), lambda i, b: (b, i, 0))
    row = pl.BlockSpec((1, tm, d), lambda i, b: (b, i, 0))
    return pl.pallas_call(
        _out_proj_kernel,
        grid=(n // tm, bsz),
        in_specs=[
            act, act, row,
            *_pos_embed_specs(pe, tm),
            pl.BlockSpec((1, N_MOD, d), lambda i, b: (b, 0, 0)),
            _const_spec((1, d)), _const_spec((1, d)),
            _const_spec(w_out.shape),
        ],
        out_specs=[row, row],
        out_shape=[jax.ShapeDtypeStruct((bsz, n, d), F32), jax.ShapeDtypeStruct((bsz, n, d), BF16)],
        compiler_params=_cparams(("parallel", "parallel")),
        name="out_proj",
    )(yf, yl, x, *pe, mod3, g_post.reshape(1, d), g_pre.reshape(1, d), w_out)


def _ffn_kernel(h_hbm, x1_hbm, wg_ref, wv_ref, cwg_ref, cwv_ref, cbg_ref, cbv_ref, wd_ref, mod_ref, gpost_ref,
                o_hbm, lhs_s, acc_s, xo_s, lhs_sem, x1_sem, out_sem, *, n_t, n_f):
    b, i, c = pl.program_id(0), pl.program_id(1), pl.program_id(2)
    tm = acc_s.shape[0]
    rows = lhs_s.shape[1]
    halo = (rows - tm) // 2
    t = b * n_t + i
    slot = t % 2
    last_tile = pl.num_programs(0) * n_t - 1

    def lhs_dma(bb, ii, sl, start):
        def run(src_row0, dst_row0, n_rows):
            cp = pltpu.make_async_copy(h_hbm.at[bb, pl.ds(src_row0, n_rows)],
                                       lhs_s.at[sl, pl.ds(dst_row0, n_rows)], lhs_sem.at[sl])
            if start:
                cp.start()
            else:
                cp.wait()

        pl.when(ii == 0)(lambda: run(0, halo, tm + halo))
        pl.when(ii == n_t - 1)(lambda: run((n_t - 1) * tm - halo, 0, tm + halo))
        if n_t > 2:
            pl.when(jnp.logical_and(ii > 0, ii < n_t - 1))(
                lambda: run(pl.multiple_of(ii * tm - halo, halo), 0, rows))

    def x1_copy():
        return pltpu.make_async_copy(x1_hbm.at[b, pl.ds(pl.multiple_of(i * tm, tm), tm)], xo_s, x1_sem.at[0])

    def out_copy(bb, ii):
        return pltpu.make_async_copy(xo_s, o_hbm.at[bb, pl.ds(pl.multiple_of(ii * tm, tm), tm)], out_sem.at[0])

    def conv_branch(w_ref, cw_ref, cb_ref):
        up = jnp.dot(lhs_s[slot], w_ref[...].astype(BF16), preferred_element_type=F32)
        left = FFN_CONV // 2
        y = cb_ref[...] + jnp.zeros((tm, up.shape[1]), F32)
        for k in range(FFN_CONV):
            off = k - left
            shifted = up if off == 0 else pltpu.roll(up, (-off) % rows, 0)
            y = y + shifted[halo:halo + tm, :] * cw_ref[k:k + 1, :]
        return y

    @pl.when(c == 0)
    def _():
        pl.when(t == 0)(lambda: lhs_dma(b, i, slot, True))
        lhs_dma(b, i, slot, False)
        zeros = jnp.zeros((halo, lhs_s.shape[2]), lhs_s.dtype)

        @pl.when(i == 0)
        def _():
            lhs_s[slot, 0:halo, :] = zeros

        @pl.when(i == n_t - 1)
        def _():
            lhs_s[slot, halo + tm:, :] = zeros

        acc_s[...] = jnp.zeros_like(acc_s)

    @pl.when(c == 1)
    def _():
        nxt = i + 1 < n_t
        pl.when(t < last_tile)(
            lambda: lhs_dma(jnp.where(nxt, b, b + 1), jnp.where(nxt, i + 1, 0), 1 - slot, True))
        prv = i > 0
        pl.when(t > 0)(lambda: out_copy(jnp.where(prv, b, b - 1), jnp.where(prv, i - 1, n_t - 1)).wait())
        x1_copy().start()

    g = conv_branch(wg_ref, cwg_ref, cbg_ref)
    v = conv_branch(wv_ref, cwv_ref, cbv_ref)
    act = (jax.nn.gelu(g, approximate=True) * v).astype(BF16)
    acc_s[...] += jnp.dot(act, wd_ref[...].astype(BF16), preferred_element_type=F32)

    @pl.when(c == n_f - 1)
    def _():
        x1_copy().wait()
        for r0 in range(0, tm, FFN_EPILOGUE_ROWS):
            rs = slice(r0, r0 + FFN_EPILOGUE_ROWS)
            xo_s[rs, :] = xo_s[rs, :] + mod_ref[0, 5:6, :] * _rms_norm(acc_s[rs, :], gpost_ref[...])
        out_copy(b, i).start()
        pl.when(t == last_tile)(lambda: out_copy(b, i).wait())


def _conv_ffn(h2, x1, mod3, g_post, w_up, conv_w, conv_b, w_down):
    bsz, n, d = h2.shape
    d_ff = w_down.shape[0]
    tm, tf, halo = FFN_TM, FFN_TF, FFN_HALO
    nt, nf = n // tm, d_ff // tf
    assert nt >= 2 and n % tm == 0 and d_ff % tf == 0 and tm % FFN_EPILOGUE_ROWS == 0
    gcol = lambda r: pl.BlockSpec((r, tf), lambda b, i, c: (0, c))
    vcol = lambda r: pl.BlockSpec((r, tf), lambda b, i, c: (0, c + nf))
    any_spec = pl.BlockSpec(memory_space=pl.ANY)
    conv_b2 = conv_b.reshape(1, 2 * d_ff)
    rows = tm + 2 * halo
    return pl.pallas_call(
        functools.partial(_ffn_kernel, n_t=nt, n_f=nf),
        grid=(bsz, nt, nf),
        in_specs=[
            any_spec, any_spec,
            gcol(d), vcol(d),
            gcol(FFN_CONV), vcol(FFN_CONV),
            gcol(1), vcol(1),
            pl.BlockSpec((tf, d), lambda b, i, c: (c, 0)),
            pl.BlockSpec((1, N_MOD, d), lambda b, i, c: (b, 0, 0)),
            _const_spec((1, d)),
        ],
        out_specs=any_spec,
        out_shape=jax.ShapeDtypeStruct((bsz, n, d), F32),
        scratch_shapes=[
            pltpu.VMEM((2, rows, d), BF16),
            pltpu.VMEM((tm, d), F32),
            pltpu.VMEM((tm, d), F32),
            pltpu.SemaphoreType.DMA((2,)),
            pltpu.SemaphoreType.DMA((1,)),
            pltpu.SemaphoreType.DMA((1,)),
        ],
        compiler_params=_cparams(("arbitrary", "arbitrary", "arbitrary"), FFN_VMEM_LIMIT_BYTES),
        name="conv_ffn",
    )(h2, x1, w_up, w_up, conv_w, conv_w, conv_b2, conv_b2, w_down, mod3, g_post.reshape(1, d))


def _grid_pos_tables(n_tokens, d_model):
    quarter = d_model // 4
    freqs = POS_BASE ** (-jnp.arange(quarter, dtype=F32) / quarter)

    def enc(count):
        ang = jnp.arange(count, dtype=F32)[:, None] * freqs[None, :]
        return jnp.concatenate([jnp.sin(ang), jnp.cos(ang)], axis=-1)

    return enc(n_tokens // GRID_W), enc(GRID_W)


def kernel(x, c, ctx, c_ctx, w_ada, b_ada, g_mix_pre, g_mix_post, g_ffn_pre, g_ffn_post,
           w_in, conv_lru_w, conv_lru_b, w_rec_gate, b_rec_gate, w_in_gate, b_in_gate,
           lru_lambda, w_fourier, b_fourier, w_out, w_up, conv_ffn_w, conv_ffn_b, w_down):
    bsz, n, d = x.shape
    d_fourier = w_fourier.shape[1] * w_fourier.shape[2]
    d_lru = conv_lru_w.shape[2]
    assert w_ada.shape[0] == 1, "single layer: the context stream is only read, never updated"
    assert bsz + 1 <= MOD_ROWS and d_lru // N_LRU_HEADS == LANES
    l = 0
    pe = _grid_pos_tables(n, d)

    cond = jnp.zeros((MOD_ROWS, d), F32).at[:bsz].set(c).at[bsz].set(c_ctx)
    mod3 = _modulation(cond, w_ada[l], b_ada[l]).reshape(MOD_ROWS, N_MOD, d)

    w_in_b = w_in[l].astype(BF16)
    x_cols = (d_fourier, d_fourier + d_lru)
    (uxc,) = _in_proj(ctx, None, mod3, lambda b: bsz, g_mix_pre[l], w_in_b[:, x_cols[0]:x_cols[1]],
                      ((0, d_lru),), (F32,), ctx.shape[1], "in_proj_ctx")
    uf, ux, ug = _in_proj(x, pe, mod3, lambda b: b, g_mix_pre[l], w_in_b,
                          ((0, d_fourier), x_cols, (x_cols[1], x_cols[1] + d_lru)),
                          (BF16, F32, F32), IN_TM, "in_proj")

    wg = (0.5 * jnp.concatenate([w_rec_gate[l, 0], w_in_gate[l, 0], w_rec_gate[l, 1], w_in_gate[l, 1]],
                                axis=-1)).astype(BF16)
    bg = 0.5 * jnp.stack([b_rec_gate[l, 0], b_in_gate[l, 0], b_rec_gate[l, 1], b_in_gate[l, 1]])
    y_lru = _rg_lru(ux, uxc, ug, conv_lru_w[l], conv_lru_b[l], wg, bg, lru_lambda[l])
    y_fourier = _fourier(uf, w_fourier[l], b_fourier[l])

    x1, h2 = _out_proj(y_fourier, y_lru, x, pe, mod3, g_mix_post[l], g_ffn_pre[l], w_out[l].astype(BF16))
    return _conv_ffn(h2, x1, mod3, g_ffn_post[l], w_up[l], conv_ffn_w[l], conv_ffn_b[l], w_down[l])
```

```python
import functools
import itertools
import math

import jax
import jax.numpy as jnp
from jax import lax
from jax.experimental import pallas as pl
from jax.experimental.pallas import tpu as pltpu

F32 = jnp.float32
BF16 = jnp.bfloat16

EPS = 1e-6
POS_BASE = 10000.0
GRID_W = 64
N_MOD = 6
N_FOURIER_GROUPS = 4
N_LRU_HEADS = 8
LRU_CONV = 4
LRU_C = 8.0
FFN_CONV = 3

LANES = 128
SUBLANES = 8
VMEM_BYTES = 64 * 1024 * 1024
VMEM_LIMIT_BYTES = 60000 * 1024
FFN_VMEM_LIMIT_BYTES = VMEM_BYTES - 2 * 1024 * 1024

MOD_ROWS = 16
IN_TM = 512
IN_SUB_TILES = 2
OUT_SUB_ROWS = (256, 256)
LRU_CB = 512
LRU_ROW_CHUNK = 512
SCAN_UNROLL = 4
FFN_TM = 1024
FFN_TF = 512
FFN_HALO = 16
FFN_EPILOGUE_ROWS = 256


def _cparams(sem, vmem_limit_bytes=VMEM_LIMIT_BYTES):
    return pltpu.CompilerParams(dimension_semantics=sem, vmem_limit_bytes=vmem_limit_bytes)


def _const_spec(shape):
    nd = len(shape)
    return pl.BlockSpec(shape, lambda *_: (0,) * nd, pipeline_mode=pl.Buffered(1))


def _mod_kernel(c_ref, w_ref, b_ref, o_ref):
    cv = c_ref[...]
    s = (cv * jax.nn.sigmoid(cv)).astype(BF16)
    o_ref[...] = jnp.dot(s, w_ref[...].astype(BF16), preferred_element_type=F32) + b_ref[...]


def _modulation(cond, w_ada, b_ada):
    d = cond.shape[1]
    n = w_ada.shape[1]
    tn = 1024
    return pl.pallas_call(
        _mod_kernel,
        grid=(n // tn,),
        in_specs=[
            pl.BlockSpec((MOD_ROWS, d), lambda j: (0, 0)),
            pl.BlockSpec((d, tn), lambda j: (0, j)),
            pl.BlockSpec((1, tn), lambda j: (0, j)),
        ],
        out_specs=pl.BlockSpec((MOD_ROWS, tn), lambda j: (0, j)),
        out_shape=jax.ShapeDtypeStruct((MOD_ROWS, n), F32),
        compiler_params=_cparams(("parallel",)),
        name="modulation",
    )(cond, w_ada, b_ada.reshape(1, n))


def _rms_norm(v, g):
    ms = jnp.mean(v * v, axis=-1, keepdims=True)
    return v * lax.rsqrt(ms + EPS) * g


def _tile_pos_embed(prow_ref, pcol_ref):
    rows = jnp.concatenate([jnp.broadcast_to(prow_ref[r:r + 1, :], pcol_ref.shape)
                            for r in range(prow_ref.shape[0])], axis=0)
    cols = jnp.concatenate([pcol_ref[...]] * prow_ref.shape[0], axis=0)
    return jnp.concatenate([rows, cols], axis=1)


def _in_proj_kernel(*refs, has_pe, col_splits):
    if has_pe:
        x_ref, prow_ref, pcol_ref, mod_ref, g_ref, w_ref = refs[:6]
        out_refs = refs[6:]
        pe = _tile_pos_embed(prow_ref, pcol_ref)
    else:
        x_ref, mod_ref, g_ref, w_ref = refs[:4]
        out_refs = refs[4:]
        pe = None
    n_rows = x_ref.shape[1]
    sub = n_rows // IN_SUB_TILES
    for r0 in range(0, n_rows, sub):
        rs = slice(r0, r0 + sub)
        xp = x_ref[0, rs, :] if pe is None else x_ref[0, rs, :] + pe[rs, :]
        h = _rms_norm(xp, g_ref[...]) * (1.0 + mod_ref[0, 1:2, :]) + mod_ref[0, 0:1, :]
        hb = h.astype(BF16)
        for o_ref, (c0, c1) in zip(out_refs, col_splits):
            o_ref[0, rs, :] = jnp.dot(hb, w_ref[:, c0:c1], preferred_element_type=F32).astype(o_ref.dtype)


def _pos_embed_specs(pe, tm):
    prow, pcol = pe
    assert tm % GRID_W == 0
    return [pl.BlockSpec((tm // GRID_W, prow.shape[1]), lambda i, b: (i, 0)), _const_spec(pcol.shape)]


def _in_proj(x, pe, mod3, mod_row, g, w, col_splits, out_dtypes, tm, name):
    bsz, n, d = x.shape
    has_pe = pe is not None
    in_specs = [pl.BlockSpec((1, tm, d), lambda i, b: (b, i, 0))]
    args = [x]
    if has_pe:
        in_specs += _pos_embed_specs(pe, tm)
        args += list(pe)
    in_specs += [
        pl.BlockSpec((1, N_MOD, d), lambda i, b: (mod_row(b), 0, 0)),
        _const_spec((1, d)),
        _const_spec(w.shape),
    ]
    args += [mod3, g.reshape(1, d), w]
    out_specs = [pl.BlockSpec((1, tm, c1 - c0), lambda i, b: (b, i, 0)) for c0, c1 in col_splits]
    out_shape = [jax.ShapeDtypeStruct((bsz, n, c1 - c0), dt) for (c0, c1), dt in zip(col_splits, out_dtypes)]
    return pl.pallas_call(
        functools.partial(_in_proj_kernel, has_pe=has_pe, col_splits=col_splits),
        grid=(n // tm, bsz),
        in_specs=in_specs,
        out_specs=out_specs,
        out_shape=out_shape,
        compiler_params=_cparams(("parallel", "parallel")),
        name=name,
    )(*args)


def _scan_pitch(n):
    m = -(-n // SUBLANES)
    while m % 8 != 4:
        m += 1
    return m


def _depthwise_conv_rows(x, w_ref, b_ref, left):
    n = x.shape[0]
    k_w = w_ref.shape[0]
    row = lax.broadcasted_iota(jnp.int32, x.shape, 0)
    y = b_ref[...] + jnp.zeros_like(x)
    for k in range(k_w):
        off = k - left
        if off == 0:
            xs = x
        elif off < 0:
            xs = jnp.where(row >= -off, pltpu.roll(x, -off, 0), 0.0)
        else:
            xs = jnp.where(row < n - off, pltpu.roll(x, n - off, 0), 0.0)
        y = y + xs * w_ref[k:k + 1, :]
    return y


def _log_sigmoid(z):
    return jnp.minimum(z, 0.0) - jnp.log1p(jnp.exp(-jnp.abs(z)))


def _lru_gates(xc_ref, n, row_chunk, wg_ref, bg_ref, quarter_c_ls, a_ref, b_ref, n_slabs):
    for r0 in range(0, n, row_chunk):
        for s in range(n_slabs):
            lanes = slice(s * LANES, (s + 1) * LANES)
            xh = xc_ref[r0:r0 + row_chunk, lanes]
            g = jnp.dot(xh.astype(BF16), wg_ref[s], preferred_element_type=F32)
            for d in range(2):
                tr = jnp.tanh(g[:, (2 * d) * LANES:(2 * d + 1) * LANES] + bg_ref[2 * d:2 * d + 1, lanes])
                ti = jnp.tanh(g[:, (2 * d + 1) * LANES:(2 * d + 2) * LANES] + bg_ref[2 * d + 1:2 * d + 2, lanes])
                hl = quarter_c_ls[d:d + 1, lanes]
                t = jnp.tanh(hl + hl * tr)
                q = pl.reciprocal(1.0 - t, full_range=False)
                nt = -t
                root = jnp.where(nt > 0.0, nt * lax.rsqrt(nt), 0.0)
                a_ref[d, s, r0:r0 + row_chunk, :] = (1.0 + t) * q
                b_ref[d, s, r0:r0 + row_chunk, :] = (root * q) * ((1.0 + ti) * xh)


def _lru_scan(a_ref, b_ref, n, h0, n_slabs, h_ref=None):
    m = _scan_pitch(n)
    pairs = [(d, s) for d in range(2) for s in range(n_slabs)]
    sub = lax.broadcasted_iota(jnp.int32, (SUBLANES, LANES), 0)

    def rows_of(d, j):
        return pl.ds(j if d == 0 else m - 1 - j, SUBLANES, stride=m)

    def summary_body(j, carry):
        out = []
        for (d, s), (h, p) in zip(pairs, carry):
            a = a_ref[d, s, rows_of(d, j), :]
            out.append((a * h + b_ref[d, s, rows_of(d, j), :], a * p))
        return tuple(out)

    init = tuple((jnp.zeros((SUBLANES, LANES), F32), jnp.ones((SUBLANES, LANES), F32)) for _ in pairs)
    local = lax.fori_loop(0, m, summary_body, init, unroll=SCAN_UNROLL)

    carries, finals = [], []
    for (d, s), (hfin, ptot) in zip(pairs, local):
        first, last, shift = (0, SUBLANES - 1, 1) if d == 0 else (SUBLANES - 1, 0, SUBLANES - 1)
        hin = jnp.where(sub == first, jnp.broadcast_to(h0[d][s], (SUBLANES, LANES)), 0.0)
        for step in range(1, SUBLANES):
            k = first + step if d == 0 else first - step
            nxt = pltpu.roll(hfin + ptot * hin, shift, 0)
            hin = jnp.where(sub == k, nxt, hin)
        carries.append(hin)
        finals.append((hfin + ptot * hin)[last:last + 1, :])

    if h_ref is not None:
        def state_body(j, carry):
            out = []
            for (d, s), h in zip(pairs, carry):
                h = a_ref[d, s, rows_of(d, j), :] * h + b_ref[d, s, rows_of(d, j), :]
                h_ref[d, s, rows_of(d, j), :] = h
                out.append(h)
            return tuple(out)

        lax.fori_loop(0, m, state_body, tuple(carries), unroll=SCAN_UNROLL)

    return [[finals[d * n_slabs + s] for s in range(n_slabs)] for d in range(2)]


def _lru_kernel(ux_ref, uxc_ref, ug_ref, cw_ref, cb_ref, wg_ref, bg_ref, lam_ref, o_ref,
                xc_s, xcc_s, a_s, b_s, ac_s, bc_s, h_s):
    n = ux_ref.shape[1]
    n_ctx = uxc_ref.shape[1]
    n_slabs = ux_ref.shape[2] // LANES
    ls = (0.25 * LRU_C) * _log_sigmoid(lam_ref[...])

    for ref, rows in ((a_s, n), (ac_s, n_ctx)):
        ref[:, :, rows:, :] = jnp.ones((2, n_slabs, ref.shape[2] - rows, LANES), F32)
    for ref, rows in ((b_s, n), (bc_s, n_ctx)):
        ref[:, :, rows:, :] = jnp.zeros((2, n_slabs, ref.shape[2] - rows, LANES), F32)

    left = LRU_CONV // 2
    for s in range(n_slabs):
        lanes = slice(s * LANES, (s + 1) * LANES)
        taps = (cw_ref.at[:, lanes], cb_ref.at[:, lanes], left)
        xcc_s[:, lanes] = _depthwise_conv_rows(uxc_ref[0, :, lanes], *taps)
        xc_s[:, lanes] = _depthwise_conv_rows(ux_ref[0, :, lanes], *taps)

    _lru_gates(xcc_s, n_ctx, n_ctx, wg_ref, bg_ref, ls, ac_s, bc_s, n_slabs)
    _lru_gates(xc_s, n, LRU_ROW_CHUNK, wg_ref, bg_ref, ls, a_s, b_s, n_slabs)

    zero = jnp.zeros((1, LANES), F32)
    h0 = _lru_scan(ac_s, bc_s, n_ctx, [[zero] * n_slabs] * 2, n_slabs)
    _lru_scan(a_s, b_s, n, h0, n_slabs, h_s)

    for r0 in range(0, n, LRU_ROW_CHUNK):
        rows = slice(r0, r0 + LRU_ROW_CHUNK)
        for s in range(n_slabs):
            lanes = slice(s * LANES, (s + 1) * LANES)
            hsum = h_s[0, s, rows, :] + h_s[1, s, rows, :]
            gate = jax.nn.gelu(ug_ref[0, rows, lanes], approximate=True)
            o_ref[0, rows, lanes] = (hsum * gate).astype(o_ref.dtype)


def _rg_lru(ux, uxc, ug, conv_w, conv_b, wg, bg, lam):
    bsz, n, c = ux.shape
    n_ctx = uxc.shape[1]
    cb = LRU_CB
    n_slabs = cb // LANES
    lp = SUBLANES * _scan_pitch(n)
    cp = SUBLANES * _scan_pitch(n_ctx)
    seq = lambda rows: pl.BlockSpec((1, rows, cb), lambda b, j: (b, 0, j))
    par = lambda rows: pl.BlockSpec((rows, cb), lambda b, j: (0, j))
    return pl.pallas_call(
        _lru_kernel,
        grid=(bsz, c // cb),
        in_specs=[
            seq(n), seq(n_ctx), seq(n),
            par(conv_w.shape[0]), par(1),
            pl.BlockSpec((n_slabs, LANES, 4 * LANES), lambda b, j: (j, 0, 0)),
            par(4), par(2),
        ],
        out_specs=seq(n),
        out_shape=jax.ShapeDtypeStruct((bsz, n, c), BF16),
        scratch_shapes=[
            pltpu.VMEM((n, cb), F32),
            pltpu.VMEM((n_ctx, cb), F32),
            pltpu.VMEM((2, n_slabs, lp, LANES), F32),
            pltpu.VMEM((2, n_slabs, lp, LANES), F32),
            pltpu.VMEM((2, n_slabs, cp, LANES), F32),
            pltpu.VMEM((2, n_slabs, cp, LANES), F32),
            pltpu.VMEM((2, n_slabs, lp, LANES), F32),
        ],
        compiler_params=_cparams(("parallel", "parallel")),
        name="rg_lru",
    )(ux, uxc, ug, conv_w, conv_b.reshape(1, c), wg, bg, lam)


FFT_RADIX = 8
_FFT_BLOCK_OF = {0: (0, 0.0), 4: (4, 0.0), 2: (2, -1.0), 6: (2, 1.0), 1: (1, -1.0), 7: (1, 1.0), 3: (3, 1.0), 5: (3, -1.0)}


def _fourier_kernel(u_ref, m_ref, ccs_ref, wf_ref, bf_ref, o_ref, t_s, z_s, y_s):
    n = u_ref.shape[1]
    r = n // FFT_RADIX
    groups, gw, _ = wf_ref.shape
    x = [u_ref[0, p * r:(p + 1) * r, :].astype(F32) for p in range(FFT_RADIX)]
    e0, e1, e2, e3 = x[0] + x[4], x[0] - x[4], x[2] + x[6], x[2] - x[6]
    o0, o1, o2, o3 = x[1] + x[5], x[1] - x[5], x[3] + x[7], x[3] - x[7]
    ee, oo = e0 + e2, o0 + o2
    c = math.sqrt(0.5)
    p, m = c * (o1 - o3), c * (o1 + o3)
    t_re = {0: ee + oo, 4: ee - oo, 2: e0 - e2, 1: e1 + p, 3: e1 - p}
    t_im = {2: o0 - o2, 1: m + e3, 3: e3 - m}

    for blk in t_re:
        t_s[blk, 0] = t_re[blk].astype(BF16)
        if blk in t_im:
            t_s[blk, 1] = t_im[blk].astype(BF16)

    for k1 in range(FFT_RADIX):
        blk = _FFT_BLOCK_OF[k1][0]
        if blk in t_im:
            z = jnp.dot(m_ref[k1], t_s[blk].reshape(2 * r, -1), preferred_element_type=F32)
        else:
            z = jnp.dot(m_ref[k1, :, :r], t_s[blk, 0], preferred_element_type=F32)
        z_s[k1] = z.reshape(2, r, -1).astype(BF16)

    for g in range(groups):
        cols = slice(g * gw, (g + 1) * gw)
        f = jnp.dot(z_s[:, 0, :, cols].reshape(n, gw), ccs_ref[0], preferred_element_type=F32)
        f = f + jnp.dot(z_s[:, 1, :, cols].reshape(n, gw), ccs_ref[1], preferred_element_type=F32)
        y = jnp.dot(f.astype(BF16), wf_ref[g], preferred_element_type=F32) + bf_ref[:, cols]
        for k1 in range(FFT_RADIX):
            for h in range(gw // LANES):
                slab = g * (gw // LANES) + h
                y_s[slab, pl.ds(k1, r, stride=FFT_RADIX), :] = y[k1 * r:(k1 + 1) * r, h * LANES:(h + 1) * LANES]

    for slab in range(y_s.shape[0]):
        o_ref[0, :, slab * LANES:(slab + 1) * LANES] = y_s[slab].astype(o_ref.dtype)


def _fourier_tables(n, gw):
    r = n // FFT_RADIX
    k = jnp.arange(FFT_RADIX, dtype=jnp.int32)[:, None, None] + FFT_RADIX * jnp.arange(r, dtype=jnp.int32)[None, :, None]
    p2 = jnp.arange(r, dtype=jnp.int32)[None, None, :]
    ang = ((k * p2) % n).astype(F32) * (2.0 * math.pi / n)
    cn, sn = jnp.cos(ang) * n ** -0.5, jnp.sin(ang) * n ** -0.5
    sign = jnp.array([_FFT_BLOCK_OF[k1][1] for k1 in range(FFT_RADIX)], F32)[:, None, None]
    m = jnp.concatenate([jnp.concatenate([cn, sign * sn], axis=2),
                         jnp.concatenate([-sn, sign * cn], axis=2)], axis=1)
    idx = jnp.arange(gw, dtype=jnp.int32)
    ang_c = ((idx[:, None] * idx[None, :]) % gw).astype(F32) * (2.0 * math.pi / gw)
    ccs = jnp.stack([jnp.cos(ang_c), jnp.sin(ang_c)]) * gw ** -0.5
    return m.astype(BF16), ccs.astype(BF16)


def _fourier(uf, w_f, b_f):
    bsz, n, c = uf.shape
    groups, gw, _ = w_f.shape
    assert n % FFT_RADIX == 0 and gw % LANES == 0
    m, ccs = _fourier_tables(n, gw)
    return pl.pallas_call(
        _fourier_kernel,
        grid=(bsz,),
        in_specs=[
            pl.BlockSpec((1, n, c), lambda b: (b, 0, 0)),
            _const_spec(m.shape),
            _const_spec(ccs.shape),
            _const_spec(w_f.shape),
            _const_spec((1, c)),
        ],
        out_specs=pl.BlockSpec((1, n, c), lambda b: (b, 0, 0)),
        out_shape=jax.ShapeDtypeStruct((bsz, n, c), BF16),
        scratch_shapes=[
            pltpu.VMEM((FFT_RADIX // 2 + 1, 2, n // FFT_RADIX, c), BF16),
            pltpu.VMEM((FFT_RADIX, 2, n // FFT_RADIX, c), BF16),
            pltpu.VMEM((c // LANES, n, LANES), F32),
        ],
        compiler_params=_cparams(("parallel",)),
        name="fourier",
    )(uf, m, ccs, w_f.astype(BF16), b_f.reshape(1, c))


def _out_proj_kernel(yf_ref, yl_ref, x_ref, prow_ref, pcol_ref, mod_ref, gpost_ref, gpre_ref, w_ref,
                     x1_ref, h2_ref):
    k = yf_ref.shape[2]
    pe = _tile_pos_embed(prow_ref, pcol_ref)
    assert sum(OUT_SUB_ROWS) == x_ref.shape[1]
    for r0, n_rows in zip(itertools.accumulate((0,) + OUT_SUB_ROWS), OUT_SUB_ROWS):
        rs = slice(r0, r0 + n_rows)
        y = jnp.dot(yf_ref[0, rs, :], w_ref[:k, :], preferred_element_type=F32)
        y = y + jnp.dot(yl_ref[0, rs, :], w_ref[k:, :], preferred_element_type=F32)
        x1 = (x_ref[0, rs, :] + pe[rs, :]) + mod_ref[0, 2:3, :] * _rms_norm(y, gpost_ref[...])
        x1_ref[0, rs, :] = x1
        h2 = _rms_norm(x1, gpre_ref[...]) * (1.0 + mod_ref[0, 4:5, :]) + mod_ref[0, 3:4, :]
        h2_ref[0, rs, :] = h2.astype(h2_ref.dtype)


def _out_proj(yf, yl, x, pe, mod3, g_post, g_pre, w_out):
    bsz, n, d = x.shape
    k = yf.shape[2]
    tm = IN_TM
    act = pl.BlockSpec((1, tm, k), lambda i, b: (b, i, 0))
    row = pl.BlockSpec((1, tm, d), lambda i, b: (b, i, 0))
    return pl.pallas_call(
        _out_proj_kernel,
        grid=(n // tm, bsz),
        in_specs=[
            act, act, row,
            *_pos_embed_specs(pe, tm),
            pl.BlockSpec((1, N_MOD, d), lambda i, b: (b, 0, 0)),
            _const_spec((1, d)), _const_spec((1, d)),
            _const_spec(w_out.shape),
        ],
        out_specs=[row, row],
        out_shape=[jax.ShapeDtypeStruct((bsz, n, d), F32), jax.ShapeDtypeStruct((bsz, n, d), BF16)],
        compiler_params=_cparams(("parallel", "parallel")),
        name="out_proj",
    )(yf, yl, x, *pe, mod3, g_post.reshape(1, d), g_pre.reshape(1, d), w_out)


def _ffn_kernel(h_hbm, x1_hbm, wg_ref, wv_ref, cwg_ref, cwv_ref, cbg_ref, cbv_ref, wd_ref, mod_ref, gpost_ref,
                o_hbm, lhs_s, acc_s, xo_s, lhs_sem, x1_sem, out_sem, *, n_t, n_f):
    b, i, c = pl.program_id(0), pl.program_id(1), pl.program_id(2)
    tm = acc_s.shape[0]
    rows = lhs_s.shape[1]
    halo = (rows - tm) // 2
    t = b * n_t + i
    slot = t % 2
    last_tile = pl.num_programs(0) * n_t - 1

    def lhs_dma(bb, ii, sl, start):
        def run(src_row0, dst_row0, n_rows):
            cp = pltpu.make_async_copy(h_hbm.at[bb, pl.ds(src_row0, n_rows)],
                                       lhs_s.at[sl, pl.ds(dst_row0, n_rows)], lhs_sem.at[sl])
            if start:
                cp.start()
            else:
                cp.wait()

        pl.when(ii == 0)(lambda: run(0, halo, tm + halo))
        pl.when(ii == n_t - 1)(lambda: run((n_t - 1) * tm - halo, 0, tm + halo))
        if n_t > 2:
            pl.when(jnp.logical_and(ii > 0, ii < n_t - 1))(
                lambda: run(pl.multiple_of(ii * tm - halo, halo), 0, rows))

    def x1_copy():
        return pltpu.make_async_copy(x1_hbm.at[b, pl.ds(pl.multiple_of(i * tm, tm), tm)], xo_s, x1_sem.at[0])

    def out_copy(bb, ii):
        return pltpu.make_async_copy(xo_s, o_hbm.at[bb, pl.ds(pl.multiple_of(ii * tm, tm), tm)], out_sem.at[0])

    def conv_branch(w_ref, cw_ref, cb_ref):
        up = jnp.dot(lhs_s[slot], w_ref[...].astype(BF16), preferred_element_type=F32)
        left = FFN_CONV // 2
        y = cb_ref[...] + jnp.zeros((tm, up.shape[1]), F32)
        for k in range(FFN_CONV):
            off = k - left
            shifted = up if off == 0 else pltpu.roll(up, (-off) % rows, 0)
            y = y + shifted[halo:halo + tm, :] * cw_ref[k:k + 1, :]
        return y

    @pl.when(c == 0)
    def _():
        pl.when(t == 0)(lambda: lhs_dma(b, i, slot, True))
        lhs_dma(b, i, slot, False)
        zeros = jnp.zeros((halo, lhs_s.shape[2]), lhs_s.dtype)

        @pl.when(i == 0)
        def _():
            lhs_s[slot, 0:halo, :] = zeros

        @pl.when(i == n_t - 1)
        def _():
            lhs_s[slot, halo + tm:, :] = zeros

        acc_s[...] = jnp.zeros_like(acc_s)

    @pl.when(c == 1)
    def _():
        nxt = i + 1 < n_t
        pl.when(t < last_tile)(
            lambda: lhs_dma(jnp.where(nxt, b, b + 1), jnp.where(nxt, i + 1, 0), 1 - slot, True))
        prv = i > 0
        pl.when(t > 0)(lambda: out_copy(jnp.where(prv, b, b - 1), jnp.where(prv, i - 1, n_t - 1)).wait())
        x1_copy().start()

    g = conv_branch(wg_ref, cwg_ref, cbg_ref)
    v = conv_branch(wv_ref, cwv_ref, cbv_ref)
    act = (jax.nn.gelu(g, approximate=True) * v).astype(BF16)
    acc_s[...] += jnp.dot(act, wd_ref[...].astype(BF16), preferred_element_type=F32)

    @pl.when(c == n_f - 1)
    def _():
        x1_copy().wait()
        for r0 in range(0, tm, FFN_EPILOGUE_ROWS):
            rs = slice(r0, r0 + FFN_EPILOGUE_ROWS)
            xo_s[rs, :] = xo_s[rs, :] + mod_ref[0, 5:6, :] * _rms_norm(acc_s[rs, :], gpost_ref[...])
        out_copy(b, i).start()
        pl.when(t == last_tile)(lambda: out_copy(b, i).wait())


def _conv_ffn(h2, x1, mod3, g_post, w_up, conv_w, conv_b, w_down):
    bsz, n, d = h2.shape
    d_ff = w_down.shape[0]
    tm, tf, halo = FFN_TM, FFN_TF, FFN_HALO
    nt, nf = n // tm, d_ff // tf
    assert nt >= 2 and n % tm == 0 and d_ff % tf == 0 and tm % FFN_EPILOGUE_ROWS == 0
    gcol = lambda r: pl.BlockSpec((r, tf), lambda b, i, c: (0, c))
    vcol = lambda r: pl.BlockSpec((r, tf), lambda b, i, c: (0, c + nf))
    any_spec = pl.BlockSpec(memory_space=pl.ANY)
    conv_b2 = conv_b.reshape(1, 2 * d_ff)
    rows = tm + 2 * halo
    return pl.pallas_call(
        functools.partial(_ffn_kernel, n_t=nt, n_f=nf),
        grid=(bsz, nt, nf),
        in_specs=[
            any_spec, any_spec,
            gcol(d), vcol(d),
            gcol(FFN_CONV), vcol(FFN_CONV),
            gcol(1), vcol(1),
            pl.BlockSpec((tf, d), lambda b, i, c: (c, 0)),
            pl.BlockSpec((1, N_MOD, d), lambda b, i, c: (b, 0, 0)),
            _const_spec((1, d)),
        ],
        out_specs=any_spec,
        out_shape=jax.ShapeDtypeStruct((bsz, n, d), F32),
        scratch_shapes=[
            pltpu.VMEM((2, rows, d), BF16),
            pltpu.VMEM((tm, d), F32),
            pltpu.VMEM((tm, d), F32),
            pltpu.SemaphoreType.DMA((2,)),
            pltpu.SemaphoreType.DMA((1,)),
            pltpu.SemaphoreType.DMA((1,)),
        ],
        compiler_params=_cparams(("arbitrary", "arbitrary", "arbitrary"), FFN_VMEM_LIMIT_BYTES),
        name="conv_ffn",
    )(h2, x1, w_up, w_up, conv_w, conv_w, conv_b2, conv_b2, w_down, mod3, g_post.reshape(1, d))


def _grid_pos_tables(n_tokens, d_model):
    quarter = d_model // 4
    freqs = POS_BASE ** (-jnp.arange(quarter, dtype=F32) / quarter)

    def enc(count):
        ang = jnp.arange(count, dtype=F32)[:, None] * freqs[None, :]
        return jnp.concatenate([jnp.sin(ang), jnp.cos(ang)], axis=-1)

    return enc(n_tokens // GRID_W), enc(GRID_W)


def kernel(x, c, ctx, c_ctx, w_ada, b_ada, g_mix_pre, g_mix_post, g_ffn_pre, g_ffn_post,
           w_in, conv_lru_w, conv_lru_b, w_rec_gate, b_rec_gate, w_in_gate, b_in_gate,
           lru_lambda, w_fourier, b_fourier, w_out, w_up, conv_ffn_w, conv_ffn_b, w_down):
    bsz, n, d = x.shape
    d_fourier = w_fourier.shape[1] * w_fourier.shape[2]
    d_lru = conv_lru_w.shape[2]
    assert w_ada.shape[0] == 1, "single layer: the context stream is only read, never updated"
    assert bsz + 1 <= MOD_ROWS and d_lru // N_LRU_HEADS == LANES
    l = 0
    pe = _grid_pos_tables(n, d)

    cond = jnp.zeros((MOD_ROWS, d), F32).at[:bsz].set(c).at[bsz].set(c_ctx)
    mod3 = _modulation(cond, w_ada[l], b_ada[l]).reshape(MOD_ROWS, N_MOD, d)

    w_in_b = w_in[l].astype(BF16)
    x_cols = (d_fourier, d_fourier + d_lru)
    n_ctx = ctx.shape[1]
    (uxc,) = _in_proj(ctx.reshape(1, bsz * n_ctx, d), None, mod3, lambda b: bsz, g_mix_pre[l],
                      w_in_b[:, x_cols[0]:x_cols[1]], ((0, d_lru),), (F32,), IN_TM, "in_proj_ctx")
    uxc = uxc.reshape(bsz, n_ctx, d_lru)
    uf, ux, ug = _in_proj(x, pe, mod3, lambda b: b, g_mix_pre[l], w_in_b,
                          ((0, d_fourier), x_cols, (x_cols[1], x_cols[1] + d_lru)),
                          (BF16, F32, F32), IN_TM, "in_proj")

    wg = (0.5 * jnp.concatenate([w_rec_gate[l, 0], w_in_gate[l, 0], w_rec_gate[l, 1], w_in_gate[l, 1]],
                                axis=-1)).astype(BF16)
    bg = 0.5 * jnp.stack([b_rec_gate[l, 0], b_in_gate[l, 0], b_rec_gate[l, 1], b_in_gate[l, 1]])
    y_lru = _rg_lru(ux, uxc, ug, conv_lru_w[l], conv_lru_b[l], wg, bg, lru_lambda[l])
    y_fourier = _fourier(uf, w_fourier[l], b_fourier[l])

    x1, h2 = _out_proj(y_fourier, y_lru, x, pe, mod3, g_mix_post[l], g_ffn_pre[l], w_out[l].astype(BF16))
    return _conv_ffn(h2, x1, mod3, g_ffn_post[l], w_up[l], conv_ffn_w[l], conv_ffn_b[l], w_down[l])
```

```python
import functools
import itertools
import math

import numpy as np
import jax
import jax.numpy as jnp
from jax import lax
from jax.experimental import pallas as pl
from jax.experimental.pallas import tpu as pltpu

F32 = jnp.float32
BF16 = jnp.bfloat16

EPS = 1e-6
POS_BASE = 10000.0
GRID_W = 64
N_MOD = 6
N_FOURIER_GROUPS = 4
N_LRU_HEADS = 8
LRU_CONV = 4
LRU_C = 8.0
FFN_CONV = 3

LANES = 128
SUBLANES = 8
VMEM_BYTES = 64 * 1024 * 1024
VMEM_LIMIT_BYTES = 60000 * 1024
FFN_VMEM_LIMIT_BYTES = VMEM_BYTES - 2 * 1024 * 1024

MOD_ROWS = 16
IN_TM = 512
IN_SUB_TILES = 2
OUT_SUB_ROWS = (256, 256)
LRU_CB = 512
LRU_ROW_CHUNK = 512
SCAN_UNROLL = 4
FFN_TM = 1024
FFN_TF = 512
FFN_HALO = 16
FFN_EPILOGUE_ROWS = 256


def _cparams(sem, vmem_limit_bytes=VMEM_LIMIT_BYTES):
    return pltpu.CompilerParams(dimension_semantics=sem, vmem_limit_bytes=vmem_limit_bytes)


def _const_spec(shape):
    nd = len(shape)
    return pl.BlockSpec(shape, lambda *_: (0,) * nd, pipeline_mode=pl.Buffered(1))


def _mod_kernel(c_ref, w_ref, b_ref, o_ref):
    cv = c_ref[...]
    s = (cv * jax.nn.sigmoid(cv)).astype(BF16)
    o_ref[...] = jnp.dot(s, w_ref[...].astype(BF16), preferred_element_type=F32) + b_ref[...]


def _modulation(cond, w_ada, b_ada):
    d = cond.shape[1]
    n = w_ada.shape[1]
    tn = 1024
    return pl.pallas_call(
        _mod_kernel,
        grid=(n // tn,),
        in_specs=[
            pl.BlockSpec((MOD_ROWS, d), lambda j: (0, 0)),
            pl.BlockSpec((d, tn), lambda j: (0, j)),
            pl.BlockSpec((1, tn), lambda j: (0, j)),
        ],
        out_specs=pl.BlockSpec((MOD_ROWS, tn), lambda j: (0, j)),
        out_shape=jax.ShapeDtypeStruct((MOD_ROWS, n), F32),
        compiler_params=_cparams(("parallel",)),
        name="modulation",
    )(cond, w_ada, b_ada.reshape(1, n))


def _rms_norm(v, g):
    ms = jnp.mean(v * v, axis=-1, keepdims=True)
    return v * lax.rsqrt(ms + EPS) * g


def _tile_pos_embed(prow_ref, pcol_ref):
    rows = jnp.concatenate([jnp.broadcast_to(prow_ref[r:r + 1, :], pcol_ref.shape)
                            for r in range(prow_ref.shape[0])], axis=0)
    cols = jnp.concatenate([pcol_ref[...]] * prow_ref.shape[0], axis=0)
    return jnp.concatenate([rows, cols], axis=1)


def _in_proj_kernel(*refs, has_pe, col_splits):
    if has_pe:
        x_ref, prow_ref, pcol_ref, mod_ref, g_ref, w_ref = refs[:6]
        out_refs = refs[6:]
        pe = _tile_pos_embed(prow_ref, pcol_ref)
    else:
        x_ref, mod_ref, g_ref, w_ref = refs[:4]
        out_refs = refs[4:]
        pe = None
    n_rows = x_ref.shape[1]
    sub = n_rows // IN_SUB_TILES
    gain = g_ref[...] * (1.0 + mod_ref[0, 1:2, :])
    for r0 in range(0, n_rows, sub):
        rs = slice(r0, r0 + sub)
        xp = x_ref[0, rs, :] if pe is None else x_ref[0, rs, :] + pe[rs, :]
        h = _rms_norm(xp, gain) + mod_ref[0, 0:1, :]
        hb = h.astype(BF16)
        for o_ref, (c0, c1) in zip(out_refs, col_splits):
            o_ref[0, rs, :] = jnp.dot(hb, w_ref[:, c0:c1], preferred_element_type=F32).astype(o_ref.dtype)


def _pos_embed_specs(pe, tm):
    prow, pcol = pe
    assert tm % GRID_W == 0
    return [pl.BlockSpec((tm // GRID_W, prow.shape[1]), lambda i, b: (i, 0)), _const_spec(pcol.shape)]


def _in_proj(x, pe, mod3, mod_row, g, w, col_splits, out_dtypes, tm, name):
    bsz, n, d = x.shape
    has_pe = pe is not None
    in_specs = [pl.BlockSpec((1, tm, d), lambda i, b: (b, i, 0))]
    args = [x]
    if has_pe:
        in_specs += _pos_embed_specs(pe, tm)
        args += list(pe)
    in_specs += [
        pl.BlockSpec((1, N_MOD, d), lambda i, b: (mod_row(b), 0, 0)),
        _const_spec((1, d)),
        _const_spec(w.shape),
    ]
    args += [mod3, g.reshape(1, d), w]
    out_specs = [pl.BlockSpec((1, tm, c1 - c0), lambda i, b: (b, i, 0)) for c0, c1 in col_splits]
    out_shape = [jax.ShapeDtypeStruct((bsz, n, c1 - c0), dt) for (c0, c1), dt in zip(col_splits, out_dtypes)]
    return pl.pallas_call(
        functools.partial(_in_proj_kernel, has_pe=has_pe, col_splits=col_splits),
        grid=(n // tm, bsz),
        in_specs=in_specs,
        out_specs=out_specs,
        out_shape=out_shape,
        compiler_params=_cparams(("parallel", "parallel")),
        name=name,
    )(*args)


def _scan_pitch(n):
    m = -(-n // SUBLANES)
    while m % 8 != 4:
        m += 1
    return m


def _depthwise_conv_rows(x, w_ref, b_ref, left):
    n = x.shape[0]
    k_w = w_ref.shape[0]
    row = lax.broadcasted_iota(jnp.int32, x.shape, 0)
    y = b_ref[...] + jnp.zeros_like(x)
    for k in range(k_w):
        off = k - left
        if off == 0:
            xs = x
        elif off < 0:
            xs = jnp.where(row >= -off, pltpu.roll(x, -off, 0), 0.0)
        else:
            xs = jnp.where(row < n - off, pltpu.roll(x, n - off, 0), 0.0)
        y = y + xs * w_ref[k:k + 1, :]
    return y


def _log_sigmoid(z):
    return jnp.minimum(z, 0.0) - jnp.log1p(jnp.exp(-jnp.abs(z)))


def _lru_gates(xc_ref, n, row_chunk, wg_ref, bg_ref, quarter_c_ls, a_ref, b_ref, n_slabs):
    for r0 in range(0, n, row_chunk):
        for s in range(n_slabs):
            lanes = slice(s * LANES, (s + 1) * LANES)
            xh = xc_ref[r0:r0 + row_chunk, lanes]
            g = jnp.dot(xh.astype(BF16), wg_ref[s], preferred_element_type=F32)
            for d in range(2):
                tr = jnp.tanh(g[:, (2 * d) * LANES:(2 * d + 1) * LANES] + bg_ref[2 * d:2 * d + 1, lanes])
                ti = jnp.tanh(g[:, (2 * d + 1) * LANES:(2 * d + 2) * LANES] + bg_ref[2 * d + 1:2 * d + 2, lanes])
                hl = quarter_c_ls[d:d + 1, lanes]
                t = jnp.tanh(hl + hl * tr)
                q = pl.reciprocal(1.0 - t, full_range=False)
                nt = -t
                root = jnp.where(nt > 0.0, nt * lax.rsqrt(nt), 0.0)
                a_ref[d, s, r0:r0 + row_chunk, :] = (1.0 + t) * q
                b_ref[d, s, r0:r0 + row_chunk, :] = (root * q) * ((1.0 + ti) * xh)


def _lru_scan(a_ref, b_ref, n, h0, n_slabs, h_ref=None):
    m = _scan_pitch(n)
    pairs = [(d, s) for d in range(2) for s in range(n_slabs)]
    sub = lax.broadcasted_iota(jnp.int32, (SUBLANES, LANES), 0)

    def rows_of(d, j):
        return pl.ds(j if d == 0 else m - 1 - j, SUBLANES, stride=m)

    def summary_body(j, carry):
        out = []
        for (d, s), (h, p) in zip(pairs, carry):
            a = a_ref[d, s, rows_of(d, j), :]
            out.append((a * h + b_ref[d, s, rows_of(d, j), :], a * p))
        return tuple(out)

    init = tuple((jnp.zeros((SUBLANES, LANES), F32), jnp.ones((SUBLANES, LANES), F32)) for _ in pairs)
    local = lax.fori_loop(0, m, summary_body, init, unroll=SCAN_UNROLL)

    carries, finals = [], []
    for (d, s), (hfin, ptot) in zip(pairs, local):
        first, last, shift = (0, SUBLANES - 1, 1) if d == 0 else (SUBLANES - 1, 0, SUBLANES - 1)
        hin = jnp.where(sub == first, jnp.broadcast_to(h0[d][s], (SUBLANES, LANES)), 0.0)
        for step in range(1, SUBLANES):
            k = first + step if d == 0 else first - step
            nxt = pltpu.roll(hfin + ptot * hin, shift, 0)
            hin = jnp.where(sub == k, nxt, hin)
        carries.append(hin)
        finals.append((hfin + ptot * hin)[last:last + 1, :])

    if h_ref is not None:
        def state_body(j, carry):
            out = []
            for (d, s), h in zip(pairs, carry):
                h = a_ref[d, s, rows_of(d, j), :] * h + b_ref[d, s, rows_of(d, j), :]
                h_ref[d, s, rows_of(d, j), :] = h
                out.append(h)
            return tuple(out)

        lax.fori_loop(0, m, state_body, tuple(carries), unroll=SCAN_UNROLL)

    return [[finals[d * n_slabs + s] for s in range(n_slabs)] for d in range(2)]


def _lru_kernel(ux_ref, uxc_ref, ug_ref, cw_ref, cb_ref, wg_ref, bg_ref, lam_ref, o_ref,
                xc_s, xcc_s, a_s, b_s, ac_s, bc_s, h_s):
    n = ux_ref.shape[1]
    n_ctx = uxc_ref.shape[1]
    n_slabs = ux_ref.shape[2] // LANES
    ls = (0.25 * LRU_C) * _log_sigmoid(lam_ref[...])

    for ref, rows in ((a_s, n), (ac_s, n_ctx)):
        ref[:, :, rows:, :] = jnp.ones((2, n_slabs, ref.shape[2] - rows, LANES), F32)
    for ref, rows in ((b_s, n), (bc_s, n_ctx)):
        ref[:, :, rows:, :] = jnp.zeros((2, n_slabs, ref.shape[2] - rows, LANES), F32)

    left = LRU_CONV // 2
    for s in range(n_slabs):
        lanes = slice(s * LANES, (s + 1) * LANES)
        taps = (cw_ref.at[:, lanes], cb_ref.at[:, lanes], left)
        xcc_s[:, lanes] = _depthwise_conv_rows(uxc_ref[0, :, lanes], *taps)
        xc_s[:, lanes] = _depthwise_conv_rows(ux_ref[0, :, lanes], *taps)

    _lru_gates(xcc_s, n_ctx, n_ctx, wg_ref, bg_ref, ls, ac_s, bc_s, n_slabs)
    _lru_gates(xc_s, n, LRU_ROW_CHUNK, wg_ref, bg_ref, ls, a_s, b_s, n_slabs)

    zero = jnp.zeros((1, LANES), F32)
    h0 = _lru_scan(ac_s, bc_s, n_ctx, [[zero] * n_slabs] * 2, n_slabs)
    _lru_scan(a_s, b_s, n, h0, n_slabs, h_s)

    for r0 in range(0, n, LRU_ROW_CHUNK):
        rows = slice(r0, r0 + LRU_ROW_CHUNK)
        for s in range(n_slabs):
            lanes = slice(s * LANES, (s + 1) * LANES)
            hsum = h_s[0, s, rows, :] + h_s[1, s, rows, :]
            gate = jax.nn.gelu(ug_ref[0, rows, lanes], approximate=True)
            o_ref[0, rows, lanes] = (hsum * gate).astype(o_ref.dtype)


def _rg_lru(ux, uxc, ug, conv_w, conv_b, wg, bg, lam):
    bsz, n, c = ux.shape
    n_ctx = uxc.shape[1]
    cb = LRU_CB
    n_slabs = cb // LANES
    lp = SUBLANES * _scan_pitch(n)
    cp = SUBLANES * _scan_pitch(n_ctx)
    seq = lambda rows: pl.BlockSpec((1, rows, cb), lambda b, j: (b, 0, j))
    par = lambda rows: pl.BlockSpec((rows, cb), lambda b, j: (0, j))
    return pl.pallas_call(
        _lru_kernel,
        grid=(bsz, c // cb),
        in_specs=[
            seq(n), seq(n_ctx), seq(n),
            par(conv_w.shape[0]), par(1),
            pl.BlockSpec((n_slabs, LANES, 4 * LANES), lambda b, j: (j, 0, 0)),
            par(4), par(2),
        ],
        out_specs=seq(n),
        out_shape=jax.ShapeDtypeStruct((bsz, n, c), BF16),
        scratch_shapes=[
            pltpu.VMEM((n, cb), F32),
            pltpu.VMEM((n_ctx, cb), F32),
            pltpu.VMEM((2, n_slabs, lp, LANES), F32),
            pltpu.VMEM((2, n_slabs, lp, LANES), F32),
            pltpu.VMEM((2, n_slabs, cp, LANES), F32),
            pltpu.VMEM((2, n_slabs, cp, LANES), F32),
            pltpu.VMEM((2, n_slabs, lp, LANES), F32),
        ],
        compiler_params=_cparams(("parallel", "parallel")),
        name="rg_lru",
    )(ux, uxc, ug, conv_w, conv_b.reshape(1, c), wg, bg, lam)


FFT_RADIX = 8
_FFT_BLOCK_OF = {0: (0, 0.0), 4: (4, 0.0), 2: (2, -1.0), 6: (2, 1.0), 1: (1, -1.0), 7: (1, 1.0), 3: (3, 1.0), 5: (3, -1.0)}


def _fourier_kernel(u_ref, m_ref, ccs_ref, wf_ref, bf_ref, o_ref, t_s, z_s, y_s):
    n = u_ref.shape[1]
    r = n // FFT_RADIX
    groups, gw, _ = wf_ref.shape
    x = [u_ref[0, p * r:(p + 1) * r, :].astype(F32) for p in range(FFT_RADIX)]
    e0, e1, e2, e3 = x[0] + x[4], x[0] - x[4], x[2] + x[6], x[2] - x[6]
    o0, o1, o2, o3 = x[1] + x[5], x[1] - x[5], x[3] + x[7], x[3] - x[7]
    ee, oo = e0 + e2, o0 + o2
    c = math.sqrt(0.5)
    p, m = c * (o1 - o3), c * (o1 + o3)
    t_re = {0: ee + oo, 4: ee - oo, 2: e0 - e2, 1: e1 + p, 3: e1 - p}
    t_im = {2: o0 - o2, 1: m + e3, 3: e3 - m}

    for blk in t_re:
        t_s[blk, 0] = t_re[blk].astype(BF16)
        if blk in t_im:
            t_s[blk, 1] = t_im[blk].astype(BF16)

    for k1 in range(FFT_RADIX):
        blk = _FFT_BLOCK_OF[k1][0]
        if blk in t_im:
            z = jnp.dot(m_ref[k1], t_s[blk].reshape(2 * r, -1), preferred_element_type=F32)
        else:
            z = jnp.dot(m_ref[k1, :, :r], t_s[blk, 0], preferred_element_type=F32)
        z_s[k1] = z.reshape(2, r, -1).astype(BF16)

    for g in range(groups):
        cols = slice(g * gw, (g + 1) * gw)
        f = jnp.dot(z_s[:, 0, :, cols].reshape(n, gw), ccs_ref[0], preferred_element_type=F32)
        f = f + jnp.dot(z_s[:, 1, :, cols].reshape(n, gw), ccs_ref[1], preferred_element_type=F32)
        y = jnp.dot(f.astype(BF16), wf_ref[g], preferred_element_type=F32) + bf_ref[:, cols]
        for k1 in range(FFT_RADIX):
            for h in range(gw // LANES):
                slab = g * (gw // LANES) + h
                y_s[slab, pl.ds(k1, r, stride=FFT_RADIX), :] = y[k1 * r:(k1 + 1) * r, h * LANES:(h + 1) * LANES]

    for slab in range(y_s.shape[0]):
        o_ref[0, :, slab * LANES:(slab + 1) * LANES] = y_s[slab].astype(o_ref.dtype)


@functools.lru_cache(maxsize=None)
def _fourier_tables(n, gw):
    r = n // FFT_RADIX
    k = np.arange(FFT_RADIX)[:, None, None] + FFT_RADIX * np.arange(r)[None, :, None]
    ang = ((k * np.arange(r)[None, None, :]) % n) * (2.0 * math.pi / n)
    cn, sn = np.cos(ang) * n ** -0.5, np.sin(ang) * n ** -0.5
    sign = np.array([_FFT_BLOCK_OF[k1][1] for k1 in range(FFT_RADIX)])[:, None, None]
    m = np.concatenate([np.concatenate([cn, sign * sn], axis=2),
                        np.concatenate([-sn, sign * cn], axis=2)], axis=1)
    idx = np.arange(gw)
    ang_c = ((idx[:, None] * idx[None, :]) % gw) * (2.0 * math.pi / gw)
    ccs = np.stack([np.cos(ang_c), np.sin(ang_c)]) * gw ** -0.5
    return m.astype(np.float32), ccs.astype(np.float32)


def _fourier(uf, w_f, b_f):
    bsz, n, c = uf.shape
    groups, gw, _ = w_f.shape
    assert n % FFT_RADIX == 0 and gw % LANES == 0
    m, ccs = (jnp.asarray(t).astype(BF16) for t in _fourier_tables(n, gw))
    return pl.pallas_call(
        _fourier_kernel,
        grid=(bsz,),
        in_specs=[
            pl.BlockSpec((1, n, c), lambda b: (b, 0, 0)),
            _const_spec(m.shape),
            _const_spec(ccs.shape),
            _const_spec(w_f.shape),
            _const_spec((1, c)),
        ],
        out_specs=pl.BlockSpec((1, n, c), lambda b: (b, 0, 0)),
        out_shape=jax.ShapeDtypeStruct((bsz, n, c), BF16),
        scratch_shapes=[
            pltpu.VMEM((FFT_RADIX // 2 + 1, 2, n // FFT_RADIX, c), BF16),
            pltpu.VMEM((FFT_RADIX, 2, n // FFT_RADIX, c), BF16),
            pltpu.VMEM((c // LANES, n, LANES), F32),
        ],
        compiler_params=_cparams(("parallel",)),
        name="fourier",
    )(uf, m, ccs, w_f.astype(BF16), b_f.reshape(1, c))


def _out_proj_kernel(yf_ref, yl_ref, x_ref, prow_ref, pcol_ref, mod_ref, gpost_ref, gpre_ref, w_ref,
                     x1_ref, h2_ref):
    k = yf_ref.shape[2]
    pe = _tile_pos_embed(prow_ref, pcol_ref)
    assert sum(OUT_SUB_ROWS) == x_ref.shape[1]
    gain_post = mod_ref[0, 2:3, :] * gpost_ref[...]
    gain_pre = gpre_ref[...] * (1.0 + mod_ref[0, 4:5, :])
    for r0, n_rows in zip(itertools.accumulate((0,) + OUT_SUB_ROWS), OUT_SUB_ROWS):
        rs = slice(r0, r0 + n_rows)
        y = jnp.dot(yf_ref[0, rs, :], w_ref[:k, :], preferred_element_type=F32)
        y = y + jnp.dot(yl_ref[0, rs, :], w_ref[k:, :], preferred_element_type=F32)
        x1 = (x_ref[0, rs, :] + pe[rs, :]) + _rms_norm(y, gain_post)
        x1_ref[0, rs, :] = x1
        h2 = _rms_norm(x1, gain_pre) + mod_ref[0, 3:4, :]
        h2_ref[0, rs, :] = h2.astype(h2_ref.dtype)


def _out_proj(yf, yl, x, pe, mod3, g_post, g_pre, w_out):
    bsz, n, d = x.shape
    k = yf.shape[2]
    tm = IN_TM
    act = pl.BlockSpec((1, tm, k), lambda i, b: (b, i, 0))
    row = pl.BlockSpec((1, tm, d), lambda i, b: (b, i, 0))
    return pl.pallas_call(
        _out_proj_kernel,
        grid=(n // tm, bsz),
        in_specs=[
            act, act, row,
            *_pos_embed_specs(pe, tm),
            pl.BlockSpec((1, N_MOD, d), lambda i, b: (b, 0, 0)),
            _const_spec((1, d)), _const_spec((1, d)),
            _const_spec(w_out.shape),
        ],
        out_specs=[row, row],
        out_shape=[jax.ShapeDtypeStruct((bsz, n, d), F32), jax.ShapeDtypeStruct((bsz, n, d), BF16)],
        compiler_params=_cparams(("parallel", "parallel")),
        name="out_proj",
    )(yf, yl, x, *pe, mod3, g_post.reshape(1, d), g_pre.reshape(1, d), w_out)


def _ffn_kernel(h_hbm, x1_hbm, wg_ref, wv_ref, cwg_ref, cwv_ref, cbg_ref, cbv_ref, wd_ref, mod_ref, gpost_ref,
                o_hbm, lhs_s, acc_s, xo_s, lhs_sem, x1_sem, out_sem, *, n_t, n_f):
    b, i, c = pl.program_id(0), pl.program_id(1), pl.program_id(2)
    tm = acc_s.shape[0]
    rows = lhs_s.shape[1]
    halo = (rows - tm) // 2
    t = b * n_t + i
    slot = t % 2
    last_tile = pl.num_programs(0) * n_t - 1

    def lhs_dma(bb, ii, sl, start):
        def run(src_row0, dst_row0, n_rows):
            cp = pltpu.make_async_copy(h_hbm.at[bb, pl.ds(src_row0, n_rows)],
                                       lhs_s.at[sl, pl.ds(dst_row0, n_rows)], lhs_sem.at[sl])
            if start:
                cp.start()
            else:
                cp.wait()

        pl.when(ii == 0)(lambda: run(0, halo, tm + halo))
        pl.when(ii == n_t - 1)(lambda: run((n_t - 1) * tm - halo, 0, tm + halo))
        if n_t > 2:
            pl.when(jnp.logical_and(ii > 0, ii < n_t - 1))(
                lambda: run(pl.multiple_of(ii * tm - halo, halo), 0, rows))

    def x1_copy():
        return pltpu.make_async_copy(x1_hbm.at[b, pl.ds(pl.multiple_of(i * tm, tm), tm)], xo_s, x1_sem.at[0])

    def out_copy(bb, ii):
        return pltpu.make_async_copy(xo_s, o_hbm.at[bb, pl.ds(pl.multiple_of(ii * tm, tm), tm)], out_sem.at[0])

    def conv_branch(w_ref, cw_ref, cb_ref):
        up = jnp.dot(lhs_s[slot], w_ref[...].astype(BF16), preferred_element_type=F32)
        left = FFN_CONV // 2
        y = cb_ref[...] + jnp.zeros((tm, up.shape[1]), F32)
        for k in range(FFN_CONV):
            off = k - left
            shifted = up if off == 0 else pltpu.roll(up, (-off) % rows, 0)
            y = y + shifted[halo:halo + tm, :] * cw_ref[k:k + 1, :]
        return y

    @pl.when(c == 0)
    def _():
        pl.when(t == 0)(lambda: lhs_dma(b, i, slot, True))
        lhs_dma(b, i, slot, False)
        zeros = jnp.zeros((halo, lhs_s.shape[2]), lhs_s.dtype)

        @pl.when(i == 0)
        def _():
            lhs_s[slot, 0:halo, :] = zeros

        @pl.when(i == n_t - 1)
        def _():
            lhs_s[slot, halo + tm:, :] = zeros

    @pl.when(c == 1)
    def _():
        nxt = i + 1 < n_t
        pl.when(t < last_tile)(
            lambda: lhs_dma(jnp.where(nxt, b, b + 1), jnp.where(nxt, i + 1, 0), 1 - slot, True))
        prv = i > 0
        pl.when(t > 0)(lambda: out_copy(jnp.where(prv, b, b - 1), jnp.where(prv, i - 1, n_t - 1)).wait())
        x1_copy().start()

    def chunk_update(first):
        g = conv_branch(wg_ref, cwg_ref, cbg_ref)
        v = conv_branch(wv_ref, cwv_ref, cbv_ref)
        act = (jax.nn.gelu(g, approximate=True) * v).astype(BF16)
        down = jnp.dot(act, wd_ref[...].astype(BF16), preferred_element_type=F32)
        if first:
            acc_s[...] = down
        else:
            acc_s[...] += down

    pl.when(c == 0)(lambda: chunk_update(True))
    pl.when(c > 0)(lambda: chunk_update(False))

    @pl.when(c == n_f - 1)
    def _():
        x1_copy().wait()
        gain = mod_ref[0, 5:6, :] * gpost_ref[...]
        for r0 in range(0, tm, FFN_EPILOGUE_ROWS):
            rs = slice(r0, r0 + FFN_EPILOGUE_ROWS)
            xo_s[rs, :] = xo_s[rs, :] + _rms_norm(acc_s[rs, :], gain)
        out_copy(b, i).start()
        pl.when(t == last_tile)(lambda: out_copy(b, i).wait())


def _conv_ffn(h2, x1, mod3, g_post, w_up, conv_w, conv_b, w_down):
    bsz, n, d = h2.shape
    d_ff = w_down.shape[0]
    tm, tf, halo = FFN_TM, FFN_TF, FFN_HALO
    nt, nf = n // tm, d_ff // tf
    assert nt >= 2 and n % tm == 0 and d_ff % tf == 0 and tm % FFN_EPILOGUE_ROWS == 0
    gcol = lambda r: pl.BlockSpec((r, tf), lambda b, i, c: (0, c))
    vcol = lambda r: pl.BlockSpec((r, tf), lambda b, i, c: (0, c + nf))
    any_spec = pl.BlockSpec(memory_space=pl.ANY)
    conv_b2 = conv_b.reshape(1, 2 * d_ff)
    rows = tm + 2 * halo
    return pl.pallas_call(
        functools.partial(_ffn_kernel, n_t=nt, n_f=nf),
        grid=(bsz, nt, nf),
        in_specs=[
            any_spec, any_spec,
            gcol(d), vcol(d),
            gcol(FFN_CONV), vcol(FFN_CONV),
            gcol(1), vcol(1),
            pl.BlockSpec((tf, d), lambda b, i, c: (c, 0)),
            pl.BlockSpec((1, N_MOD, d), lambda b, i, c: (b, 0, 0)),
            _const_spec((1, d)),
        ],
        out_specs=any_spec,
        out_shape=jax.ShapeDtypeStruct((bsz, n, d), F32),
        scratch_shapes=[
            pltpu.VMEM((2, rows, d), BF16),
            pltpu.VMEM((tm, d), F32),
            pltpu.VMEM((tm, d), F32),
            pltpu.SemaphoreType.DMA((2,)),
            pltpu.SemaphoreType.DMA((1,)),
            pltpu.SemaphoreType.DMA((1,)),
        ],
        compiler_params=_cparams(("arbitrary", "arbitrary", "arbitrary"), FFN_VMEM_LIMIT_BYTES),
        name="conv_ffn",
    )(h2, x1, w_up, w_up, conv_w, conv_w, conv_b2, conv_b2, w_down, mod3, g_post.reshape(1, d))


def _grid_pos_tables(n_tokens, d_model):
    quarter = d_model // 4
    freqs = POS_BASE ** (-jnp.arange(quarter, dtype=F32) / quarter)

    def enc(count):
        ang = jnp.arange(count, dtype=F32)[:, None] * freqs[None, :]
        return jnp.concatenate([jnp.sin(ang), jnp.cos(ang)], axis=-1)

    return enc(n_tokens // GRID_W), enc(GRID_W)


def kernel(x, c, ctx, c_ctx, w_ada, b_ada, g_mix_pre, g_mix_post, g_ffn_pre, g_ffn_post,
           w_in, conv_lru_w, conv_lru_b, w_rec_gate, b_rec_gate, w_in_gate, b_in_gate,
           lru_lambda, w_fourier, b_fourier, w_out, w_up, conv_ffn_w, conv_ffn_b, w_down):
    bsz, n, d = x.shape
    d_fourier = w_fourier.shape[1] * w_fourier.shape[2]
    d_lru = conv_lru_w.shape[2]
    assert w_ada.shape[0] == 1, "single layer: the context stream is only read, never updated"
    assert bsz + 1 <= MOD_ROWS and d_lru // N_LRU_HEADS == LANES
    l = 0
    pe = _grid_pos_tables(n, d)

    cond = jnp.zeros((MOD_ROWS, d), F32).at[:bsz].set(c).at[bsz].set(c_ctx)
    mod3 = _modulation(cond, w_ada[l], b_ada[l]).reshape(MOD_ROWS, N_MOD, d)

    w_in_b = w_in[l].astype(BF16)
    x_cols = (d_fourier, d_fourier + d_lru)
    n_ctx = ctx.shape[1]
    (uxc,) = _in_proj(ctx.reshape(1, bsz * n_ctx, d), None, mod3, lambda b: bsz, g_mix_pre[l],
                      w_in_b[:, x_cols[0]:x_cols[1]], ((0, d_lru),), (F32,), IN_TM, "in_proj_ctx")
    uxc = uxc.reshape(bsz, n_ctx, d_lru)
    uf, ux, ug = _in_proj(x, pe, mod3, lambda b: b, g_mix_pre[l], w_in_b,
                          ((0, d_fourier), x_cols, (x_cols[1], x_cols[1] + d_lru)),
                          (BF16, F32, F32), IN_TM, "in_proj")

    wg = (0.5 * jnp.concatenate([w_rec_gate[l, 0], w_in_gate[l, 0], w_rec_gate[l, 1], w_in_gate[l, 1]],
                                axis=-1)).astype(BF16)
    bg = 0.5 * jnp.stack([b_rec_gate[l, 0], b_in_gate[l, 0], b_rec_gate[l, 1], b_in_gate[l, 1]])
    y_lru = _rg_lru(ux, uxc, ug, conv_lru_w[l], conv_lru_b[l], wg, bg, lru_lambda[l])
    y_fourier = _fourier(uf, w_fourier[l], b_fourier[l])

    x1, h2 = _out_proj(y_fourier, y_lru, x, pe, mod3, g_mix_post[l], g_ffn_pre[l], w_out[l].astype(BF16))
    return _conv_ffn(h2, x1, mod3, g_ffn_post[l], w_up[l], conv_ffn_w[l], conv_ffn_b[l], w_down[l])
```

```python
import functools
import itertools
import math

import numpy as np
import jax
import jax.numpy as jnp
from jax import lax
from jax.experimental import pallas as pl
from jax.experimental.pallas import tpu as pltpu

F32 = jnp.float32
BF16 = jnp.bfloat16

EPS = 1e-6
POS_BASE = 10000.0
GRID_W = 64
N_MOD = 6
N_FOURIER_GROUPS = 4
N_LRU_HEADS = 8
LRU_CONV = 4
LRU_C = 8.0
FFN_CONV = 3

LANES = 128
SUBLANES = 8
VMEM_BYTES = 64 * 1024 * 1024
VMEM_LIMIT_BYTES = 60000 * 1024
FFN_VMEM_LIMIT_BYTES = VMEM_BYTES - 2 * 1024 * 1024

MOD_ROWS = 16
IN_TM = 512
IN_SUB_TILES = 2
OUT_SUB_ROWS = (256, 256)
LRU_CB = 512
LRU_ROW_CHUNK = 512
SCAN_UNROLL = 4
FFN_TM = 1024
FFN_TF = 512
FFN_HALO = 16
FFN_EPILOGUE_ROWS = 256


def _cparams(sem, vmem_limit_bytes=VMEM_LIMIT_BYTES):
    return pltpu.CompilerParams(dimension_semantics=sem, vmem_limit_bytes=vmem_limit_bytes)


def _const_spec(shape):
    nd = len(shape)
    return pl.BlockSpec(shape, lambda *_: (0,) * nd, pipeline_mode=pl.Buffered(1))


def _mod_kernel(c_ref, w_ref, b_ref, o_ref):
    cv = c_ref[...]
    s = (cv * jax.nn.sigmoid(cv)).astype(BF16)
    o_ref[...] = jnp.dot(s, w_ref[...].astype(BF16), preferred_element_type=F32) + b_ref[...]


def _modulation(cond, w_ada, b_ada):
    d = cond.shape[1]
    n = w_ada.shape[1]
    tn = 1024
    return pl.pallas_call(
        _mod_kernel,
        grid=(n // tn,),
        in_specs=[
            pl.BlockSpec((MOD_ROWS, d), lambda j: (0, 0)),
            pl.BlockSpec((d, tn), lambda j: (0, j)),
            pl.BlockSpec((1, tn), lambda j: (0, j)),
        ],
        out_specs=pl.BlockSpec((MOD_ROWS, tn), lambda j: (0, j)),
        out_shape=jax.ShapeDtypeStruct((MOD_ROWS, n), F32),
        compiler_params=_cparams(("parallel",)),
        name="modulation",
    )(cond, w_ada, b_ada.reshape(1, n))


def _rms_norm(v, g):
    ms = jnp.mean(v * v, axis=-1, keepdims=True)
    return v * lax.rsqrt(ms + EPS) * g


def _tile_pos_embed(prow_ref, pcol_ref):
    rows = jnp.concatenate([jnp.broadcast_to(prow_ref[r:r + 1, :], pcol_ref.shape)
                            for r in range(prow_ref.shape[0])], axis=0)
    cols = jnp.concatenate([pcol_ref[...]] * prow_ref.shape[0], axis=0)
    return jnp.concatenate([rows, cols], axis=1)


def _in_proj_kernel(*refs, has_pe, col_splits):
    if has_pe:
        x_ref, prow_ref, pcol_ref, mod_ref, g_ref, w_ref = refs[:6]
        out_refs = refs[6:]
        pe = _tile_pos_embed(prow_ref, pcol_ref)
    else:
        x_ref, mod_ref, g_ref, w_ref = refs[:4]
        out_refs = refs[4:]
        pe = None
    n_rows = x_ref.shape[1]
    sub = n_rows // IN_SUB_TILES
    gain = g_ref[...] * (1.0 + mod_ref[0, 1:2, :])
    for r0 in range(0, n_rows, sub):
        rs = slice(r0, r0 + sub)
        xp = x_ref[0, rs, :] if pe is None else x_ref[0, rs, :] + pe[rs, :]
        h = _rms_norm(xp, gain) + mod_ref[0, 0:1, :]
        hb = h.astype(BF16)
        for o_ref, (c0, c1) in zip(out_refs, col_splits):
            o_ref[0, rs, :] = jnp.dot(hb, w_ref[:, c0:c1], preferred_element_type=F32).astype(o_ref.dtype)


def _pos_embed_specs(pe, tm):
    prow, pcol = pe
    assert tm % GRID_W == 0
    return [pl.BlockSpec((tm // GRID_W, prow.shape[1]), lambda i, b: (i, 0)), _const_spec(pcol.shape)]


def _in_proj(x, pe, mod3, mod_row, g, w, col_splits, out_dtypes, tm, name):
    bsz, n, d = x.shape
    has_pe = pe is not None
    in_specs = [pl.BlockSpec((1, tm, d), lambda i, b: (b, i, 0))]
    args = [x]
    if has_pe:
        in_specs += _pos_embed_specs(pe, tm)
        args += list(pe)
    in_specs += [
        pl.BlockSpec((1, N_MOD, d), lambda i, b: (mod_row(b), 0, 0)),
        _const_spec((1, d)),
        _const_spec(w.shape),
    ]
    args += [mod3, g.reshape(1, d), w]
    out_specs = [pl.BlockSpec((1, tm, c1 - c0), lambda i, b: (b, i, 0)) for c0, c1 in col_splits]
    out_shape = [jax.ShapeDtypeStruct((bsz, n, c1 - c0), dt) for (c0, c1), dt in zip(col_splits, out_dtypes)]
    return pl.pallas_call(
        functools.partial(_in_proj_kernel, has_pe=has_pe, col_splits=col_splits),
        grid=(n // tm, bsz),
        in_specs=in_specs,
        out_specs=out_specs,
        out_shape=out_shape,
        compiler_params=_cparams(("parallel", "parallel")),
        name=name,
    )(*args)


def _scan_pitch(n):
    m = -(-n // SUBLANES)
    while m % 8 != 4:
        m += 1
    return m


def _depthwise_conv_rows(x, w_ref, b_ref, left):
    n = x.shape[0]
    k_w = w_ref.shape[0]
    row = lax.broadcasted_iota(jnp.int32, x.shape, 0)
    y = b_ref[...] + jnp.zeros_like(x)
    for k in range(k_w):
        off = k - left
        if off == 0:
            xs = x
        elif off < 0:
            xs = jnp.where(row >= -off, pltpu.roll(x, -off, 0), 0.0)
        else:
            xs = jnp.where(row < n - off, pltpu.roll(x, n - off, 0), 0.0)
        y = y + xs * w_ref[k:k + 1, :]
    return y


def _log_sigmoid(z):
    return jnp.minimum(z, 0.0) - jnp.log1p(jnp.exp(-jnp.abs(z)))


def _lru_gates(xc_ref, n, row_chunk, wg_ref, bg_ref, quarter_c_ls, a_ref, b_ref, n_slabs):
    for r0 in range(0, n, row_chunk):
        for s in range(n_slabs):
            lanes = slice(s * LANES, (s + 1) * LANES)
            xh = xc_ref[r0:r0 + row_chunk, lanes]
            g = jnp.dot(xh.astype(BF16), wg_ref[s], preferred_element_type=F32)
            for d in range(2):
                tr = jnp.tanh(g[:, (2 * d) * LANES:(2 * d + 1) * LANES] + bg_ref[2 * d:2 * d + 1, lanes])
                ti = jnp.tanh(g[:, (2 * d + 1) * LANES:(2 * d + 2) * LANES] + bg_ref[2 * d + 1:2 * d + 2, lanes])
                hl = -quarter_c_ls[d:d + 1, lanes]
                t = jnp.tanh(hl + hl * tr)
                q = pl.reciprocal(1.0 + t, full_range=False)
                root = jnp.where(t > 0.0, t * lax.rsqrt(t), 0.0)
                a_ref[d, s, r0:r0 + row_chunk, :] = (1.0 - t) * q
                b_ref[d, s, r0:r0 + row_chunk, :] = (root * q) * ((1.0 + ti) * xh)


def _lru_scan(a_ref, b_ref, n, h0, n_slabs, h_ref=None):
    m = _scan_pitch(n)
    pairs = [(d, s) for d in range(2) for s in range(n_slabs)]
    sub = lax.broadcasted_iota(jnp.int32, (SUBLANES, LANES), 0)

    def rows_of(d, j):
        return pl.ds(j if d == 0 else m - 1 - j, SUBLANES, stride=m)

    def summary_body(j, carry):
        out = []
        for (d, s), (h, p) in zip(pairs, carry):
            a = a_ref[d, s, rows_of(d, j), :]
            out.append((a * h + b_ref[d, s, rows_of(d, j), :], a * p))
        return tuple(out)

    init = tuple((jnp.zeros((SUBLANES, LANES), F32), jnp.ones((SUBLANES, LANES), F32)) for _ in pairs)
    local = lax.fori_loop(0, m, summary_body, init, unroll=SCAN_UNROLL)

    carries, finals = [], []
    for (d, s), (hfin, ptot) in zip(pairs, local):
        first, last, shift = (0, SUBLANES - 1, 1) if d == 0 else (SUBLANES - 1, 0, SUBLANES - 1)
        hin = jnp.where(sub == first, jnp.broadcast_to(h0[d][s], (SUBLANES, LANES)), 0.0)
        for step in range(1, SUBLANES):
            k = first + step if d == 0 else first - step
            nxt = pltpu.roll(hfin + ptot * hin, shift, 0)
            hin = jnp.where(sub == k, nxt, hin)
        carries.append(hin)
        finals.append((hfin + ptot * hin)[last:last + 1, :])

    if h_ref is not None:
        def state_body(j, carry):
            out = []
            for (d, s), h in zip(pairs, carry):
                h = a_ref[d, s, rows_of(d, j), :] * h + b_ref[d, s, rows_of(d, j), :]
                h_ref[d, s, rows_of(d, j), :] = h
                out.append(h)
            return tuple(out)

        lax.fori_loop(0, m, state_body, tuple(carries), unroll=SCAN_UNROLL)

    return [[finals[d * n_slabs + s] for s in range(n_slabs)] for d in range(2)]


def _lru_kernel(ux_ref, uxc_ref, ug_ref, cw_ref, cb_ref, wg_ref, bg_ref, lam_ref, o_ref,
                xc_s, xcc_s, a_s, b_s, ac_s, bc_s, h_s):
    n = ux_ref.shape[1]
    n_ctx = uxc_ref.shape[1]
    n_slabs = ux_ref.shape[2] // LANES
    ls = (0.25 * LRU_C) * _log_sigmoid(lam_ref[...])

    for ref, rows in ((a_s, n), (ac_s, n_ctx)):
        ref[:, :, rows:, :] = jnp.ones((2, n_slabs, ref.shape[2] - rows, LANES), F32)
    for ref, rows in ((b_s, n), (bc_s, n_ctx)):
        ref[:, :, rows:, :] = jnp.zeros((2, n_slabs, ref.shape[2] - rows, LANES), F32)

    left = LRU_CONV // 2
    for s in range(n_slabs):
        lanes = slice(s * LANES, (s + 1) * LANES)
        taps = (cw_ref.at[:, lanes], cb_ref.at[:, lanes], left)
        xcc_s[:, lanes] = _depthwise_conv_rows(uxc_ref[0, :, lanes], *taps)
        xc_s[:, lanes] = _depthwise_conv_rows(ux_ref[0, :, lanes], *taps)

    _lru_gates(xcc_s, n_ctx, n_ctx, wg_ref, bg_ref, ls, ac_s, bc_s, n_slabs)
    _lru_gates(xc_s, n, LRU_ROW_CHUNK, wg_ref, bg_ref, ls, a_s, b_s, n_slabs)

    zero = jnp.zeros((1, LANES), F32)
    h0 = _lru_scan(ac_s, bc_s, n_ctx, [[zero] * n_slabs] * 2, n_slabs)
    _lru_scan(a_s, b_s, n, h0, n_slabs, h_s)

    for r0 in range(0, n, LRU_ROW_CHUNK):
        rows = slice(r0, r0 + LRU_ROW_CHUNK)
        for s in range(n_slabs):
            lanes = slice(s * LANES, (s + 1) * LANES)
            hsum = h_s[0, s, rows, :] + h_s[1, s, rows, :]
            gate = jax.nn.gelu(ug_ref[0, rows, lanes], approximate=True)
            o_ref[0, rows, lanes] = (hsum * gate).astype(o_ref.dtype)


def _rg_lru(ux, uxc, ug, conv_w, conv_b, wg, bg, lam):
    bsz, n, c = ux.shape
    n_ctx = uxc.shape[1]
    cb = LRU_CB
    n_slabs = cb // LANES
    lp = SUBLANES * _scan_pitch(n)
    cp = SUBLANES * _scan_pitch(n_ctx)
    seq = lambda rows: pl.BlockSpec((1, rows, cb), lambda b, j: (b, 0, j))
    par = lambda rows: pl.BlockSpec((rows, cb), lambda b, j: (0, j))
    return pl.pallas_call(
        _lru_kernel,
        grid=(bsz, c // cb),
        in_specs=[
            seq(n), seq(n_ctx), seq(n),
            par(conv_w.shape[0]), par(1),
            pl.BlockSpec((n_slabs, LANES, 4 * LANES), lambda b, j: (j, 0, 0)),
            par(4), par(2),
        ],
        out_specs=seq(n),
        out_shape=jax.ShapeDtypeStruct((bsz, n, c), BF16),
        scratch_shapes=[
            pltpu.VMEM((n, cb), F32),
            pltpu.VMEM((n_ctx, cb), F32),
            pltpu.VMEM((2, n_slabs, lp, LANES), F32),
            pltpu.VMEM((2, n_slabs, lp, LANES), F32),
            pltpu.VMEM((2, n_slabs, cp, LANES), F32),
            pltpu.VMEM((2, n_slabs, cp, LANES), F32),
            pltpu.VMEM((2, n_slabs, lp, LANES), F32),
        ],
        compiler_params=_cparams(("parallel", "parallel")),
        name="rg_lru",
    )(ux, uxc, ug, conv_w, conv_b.reshape(1, c), wg, bg, lam)


FFT_RADIX = 8
_FFT_BLOCK_OF = {0: (0, 0.0), 4: (4, 0.0), 2: (2, -1.0), 6: (2, 1.0), 1: (1, -1.0), 7: (1, 1.0), 3: (3, 1.0), 5: (3, -1.0)}


def _fourier_kernel(u_ref, m_ref, ccs_ref, wf_ref, bf_ref, o_ref, t_s, z_s, y_s):
    n = u_ref.shape[1]
    r = n // FFT_RADIX
    groups, gw, _ = wf_ref.shape
    x = [u_ref[0, p * r:(p + 1) * r, :].astype(F32) for p in range(FFT_RADIX)]
    e0, e1, e2, e3 = x[0] + x[4], x[0] - x[4], x[2] + x[6], x[2] - x[6]
    o0, o1, o2, o3 = x[1] + x[5], x[1] - x[5], x[3] + x[7], x[3] - x[7]
    ee, oo = e0 + e2, o0 + o2
    c = math.sqrt(0.5)
    p, m = c * (o1 - o3), c * (o1 + o3)
    t_re = {0: ee + oo, 4: ee - oo, 2: e0 - e2, 1: e1 + p, 3: e1 - p}
    t_im = {2: o0 - o2, 1: m + e3, 3: e3 - m}

    for blk in t_re:
        t_s[blk, 0] = t_re[blk].astype(BF16)
        if blk in t_im:
            t_s[blk, 1] = t_im[blk].astype(BF16)

    for k1 in range(FFT_RADIX):
        blk = _FFT_BLOCK_OF[k1][0]
        if blk in t_im:
            z = jnp.dot(m_ref[k1], t_s[blk].reshape(2 * r, -1), preferred_element_type=F32)
        else:
            z = jnp.dot(m_ref[k1, :, :r], t_s[blk, 0], preferred_element_type=F32)
        z_s[k1] = z.reshape(2, r, -1).astype(BF16)

    for g in range(groups):
        cols = slice(g * gw, (g + 1) * gw)
        f = jnp.dot(z_s[:, 0, :, cols].reshape(n, gw), ccs_ref[0], preferred_element_type=F32)
        f = f + jnp.dot(z_s[:, 1, :, cols].reshape(n, gw), ccs_ref[1], preferred_element_type=F32)
        y = jnp.dot(f.astype(BF16), wf_ref[g], preferred_element_type=F32) + bf_ref[:, cols]
        for k1 in range(FFT_RADIX):
            for h in range(gw // LANES):
                slab = g * (gw // LANES) + h
                y_s[slab, pl.ds(k1, r, stride=FFT_RADIX), :] = y[k1 * r:(k1 + 1) * r, h * LANES:(h + 1) * LANES]

    for slab in range(y_s.shape[0]):
        o_ref[0, :, slab * LANES:(slab + 1) * LANES] = y_s[slab].astype(o_ref.dtype)


@functools.lru_cache(maxsize=None)
def _fourier_tables(n, gw):
    r = n // FFT_RADIX
    k = np.arange(FFT_RADIX)[:, None, None] + FFT_RADIX * np.arange(r)[None, :, None]
    ang = ((k * np.arange(r)[None, None, :]) % n) * (2.0 * math.pi / n)
    cn, sn = np.cos(ang) * n ** -0.5, np.sin(ang) * n ** -0.5
    sign = np.array([_FFT_BLOCK_OF[k1][1] for k1 in range(FFT_RADIX)])[:, None, None]
    m = np.concatenate([np.concatenate([cn, sign * sn], axis=2),
                        np.concatenate([-sn, sign * cn], axis=2)], axis=1)
    idx = np.arange(gw)
    ang_c = ((idx[:, None] * idx[None, :]) % gw) * (2.0 * math.pi / gw)
    ccs = np.stack([np.cos(ang_c), np.sin(ang_c)]) * gw ** -0.5
    return m.astype(np.float32), ccs.astype(np.float32)


def _fourier(uf, w_f, b_f):
    bsz, n, c = uf.shape
    groups, gw, _ = w_f.shape
    assert n % FFT_RADIX == 0 and gw % LANES == 0
    m, ccs = (jnp.asarray(t).astype(BF16) for t in _fourier_tables(n, gw))
    return pl.pallas_call(
        _fourier_kernel,
        grid=(bsz,),
        in_specs=[
            pl.BlockSpec((1, n, c), lambda b: (b, 0, 0)),
            _const_spec(m.shape),
            _const_spec(ccs.shape),
            _const_spec(w_f.shape),
            _const_spec((1, c)),
        ],
        out_specs=pl.BlockSpec((1, n, c), lambda b: (b, 0, 0)),
        out_shape=jax.ShapeDtypeStruct((bsz, n, c), BF16),
        scratch_shapes=[
            pltpu.VMEM((FFT_RADIX // 2 + 1, 2, n // FFT_RADIX, c), BF16),
            pltpu.VMEM((FFT_RADIX, 2, n // FFT_RADIX, c), BF16),
            pltpu.VMEM((c // LANES, n, LANES), F32),
        ],
        compiler_params=_cparams(("parallel",)),
        name="fourier",
    )(uf, m, ccs, w_f.astype(BF16), b_f.reshape(1, c))


def _out_proj_kernel(yf_ref, yl_ref, x_ref, prow_ref, pcol_ref, mod_ref, gpost_ref, gpre_ref, w_ref,
                     x1_ref, h2_ref):
    k = yf_ref.shape[2]
    pe = _tile_pos_embed(prow_ref, pcol_ref)
    assert sum(OUT_SUB_ROWS) == x_ref.shape[1]
    gain_post = mod_ref[0, 2:3, :] * gpost_ref[...]
    gain_pre = gpre_ref[...] * (1.0 + mod_ref[0, 4:5, :])
    for r0, n_rows in zip(itertools.accumulate((0,) + OUT_SUB_ROWS), OUT_SUB_ROWS):
        rs = slice(r0, r0 + n_rows)
        y = jnp.dot(yf_ref[0, rs, :], w_ref[:k, :], preferred_element_type=F32)
        y = y + jnp.dot(yl_ref[0, rs, :], w_ref[k:, :], preferred_element_type=F32)
        x1 = (x_ref[0, rs, :] + pe[rs, :]) + _rms_norm(y, gain_post)
        x1_ref[0, rs, :] = x1
        h2 = _rms_norm(x1, gain_pre) + mod_ref[0, 3:4, :]
        h2_ref[0, rs, :] = h2.astype(h2_ref.dtype)


def _out_proj(yf, yl, x, pe, mod3, g_post, g_pre, w_out):
    bsz, n, d = x.shape
    k = yf.shape[2]
    tm = IN_TM
    act = pl.BlockSpec((1, tm, k), lambda i, b: (b, i, 0))
    row = pl.BlockSpec((1, tm, d), lambda i, b: (b, i, 0))
    return pl.pallas_call(
        _out_proj_kernel,
        grid=(n // tm, bsz),
        in_specs=[
            act, act, row,
            *_pos_embed_specs(pe, tm),
            pl.BlockSpec((1, N_MOD, d), lambda i, b: (b, 0, 0)),
            _const_spec((1, d)), _const_spec((1, d)),
            _const_spec(w_out.shape),
        ],
        out_specs=[row, row],
        out_shape=[jax.ShapeDtypeStruct((bsz, n, d), F32), jax.ShapeDtypeStruct((bsz, n, d), BF16)],
        compiler_params=_cparams(("parallel", "parallel")),
        name="out_proj",
    )(yf, yl, x, *pe, mod3, g_post.reshape(1, d), g_pre.reshape(1, d), w_out)


def _ffn_kernel(h_hbm, x1_hbm, wg_ref, wv_ref, cwg_ref, cwv_ref, cbg_ref, cbv_ref, wd_ref, mod_ref, gpost_ref,
                o_hbm, lhs_s, acc_s, xo_s, lhs_sem, x1_sem, out_sem, *, n_t, n_f):
    b, i, c = pl.program_id(0), pl.program_id(1), pl.program_id(2)
    tm = acc_s.shape[0]
    rows = lhs_s.shape[1]
    halo = (rows - tm) // 2
    t = b * n_t + i
    slot = t % 2
    last_tile = pl.num_programs(0) * n_t - 1

    def lhs_dma(bb, ii, sl, start):
        def run(src_row0, dst_row0, n_rows):
            cp = pltpu.make_async_copy(h_hbm.at[bb, pl.ds(src_row0, n_rows)],
                                       lhs_s.at[sl, pl.ds(dst_row0, n_rows)], lhs_sem.at[sl])
            if start:
                cp.start()
            else:
                cp.wait()

        pl.when(ii == 0)(lambda: run(0, halo, tm + halo))
        pl.when(ii == n_t - 1)(lambda: run((n_t - 1) * tm - halo, 0, tm + halo))
        if n_t > 2:
            pl.when(jnp.logical_and(ii > 0, ii < n_t - 1))(
                lambda: run(pl.multiple_of(ii * tm - halo, halo), 0, rows))

    def x1_copy():
        return pltpu.make_async_copy(x1_hbm.at[b, pl.ds(pl.multiple_of(i * tm, tm), tm)], xo_s, x1_sem.at[0])

    def out_copy(bb, ii):
        return pltpu.make_async_copy(xo_s, o_hbm.at[bb, pl.ds(pl.multiple_of(ii * tm, tm), tm)], out_sem.at[0])

    def conv_branch(w_ref, cw_ref, cb_ref):
        up = jnp.dot(lhs_s[slot], w_ref[...].astype(BF16), preferred_element_type=F32)
        left = FFN_CONV // 2
        y = cb_ref[...] + jnp.zeros((tm, up.shape[1]), F32)
        for k in range(FFN_CONV):
            off = k - left
            shifted = up if off == 0 else pltpu.roll(up, (-off) % rows, 0)
            y = y + shifted[halo:halo + tm, :] * cw_ref[k:k + 1, :]
        return y

    @pl.when(c == 0)
    def _():
        pl.when(t == 0)(lambda: lhs_dma(b, i, slot, True))
        lhs_dma(b, i, slot, False)
        zeros = jnp.zeros((halo, lhs_s.shape[2]), lhs_s.dtype)

        @pl.when(i == 0)
        def _():
            lhs_s[slot, 0:halo, :] = zeros

        @pl.when(i == n_t - 1)
        def _():
            lhs_s[slot, halo + tm:, :] = zeros

    @pl.when(c == 1)
    def _():
        nxt = i + 1 < n_t
        pl.when(t < last_tile)(
            lambda: lhs_dma(jnp.where(nxt, b, b + 1), jnp.where(nxt, i + 1, 0), 1 - slot, True))
        prv = i > 0
        pl.when(t > 0)(lambda: out_copy(jnp.where(prv, b, b - 1), jnp.where(prv, i - 1, n_t - 1)).wait())
        x1_copy().start()

    def gated_activation():
        g = conv_branch(wg_ref, cwg_ref, cbg_ref)
        v = conv_branch(wv_ref, cwv_ref, cbv_ref)
        return (jax.nn.gelu(g, approximate=True) * v).astype(BF16)

    @pl.when(c == 0)
    def _():
        acc_s[...] = jnp.dot(gated_activation(), wd_ref[...].astype(BF16), preferred_element_type=F32)

    @pl.when(jnp.logical_and(c > 0, c < n_f - 1))
    def _():
        acc_s[...] += jnp.dot(gated_activation(), wd_ref[...].astype(BF16), preferred_element_type=F32)

    @pl.when(c == n_f - 1)
    def _():
        act = gated_activation()
        wd = wd_ref[...].astype(BF16)
        x1_copy().wait()
        gain = mod_ref[0, 5:6, :] * gpost_ref[...]
        for r0 in range(0, tm, FFN_EPILOGUE_ROWS):
            rs = slice(r0, r0 + FFN_EPILOGUE_ROWS)
            y = acc_s[rs, :] + jnp.dot(act[rs, :], wd, preferred_element_type=F32)
            xo_s[rs, :] = xo_s[rs, :] + _rms_norm(y, gain)
        out_copy(b, i).start()
        pl.when(t == last_tile)(lambda: out_copy(b, i).wait())


def _conv_ffn(h2, x1, mod3, g_post, w_up, conv_w, conv_b, w_down):
    bsz, n, d = h2.shape
    d_ff = w_down.shape[0]
    tm, tf, halo = FFN_TM, FFN_TF, FFN_HALO
    nt, nf = n // tm, d_ff // tf
    assert nt >= 2 and nf >= 2 and n % tm == 0 and d_ff % tf == 0 and tm % FFN_EPILOGUE_ROWS == 0
    gcol = lambda r: pl.BlockSpec((r, tf), lambda b, i, c: (0, c))
    vcol = lambda r: pl.BlockSpec((r, tf), lambda b, i, c: (0, c + nf))
    any_spec = pl.BlockSpec(memory_space=pl.ANY)
    conv_b2 = conv_b.reshape(1, 2 * d_ff)
    rows = tm + 2 * halo
    return pl.pallas_call(
        functools.partial(_ffn_kernel, n_t=nt, n_f=nf),
        grid=(bsz, nt, nf),
        in_specs=[
            any_spec, any_spec,
            gcol(d), vcol(d),
            gcol(FFN_CONV), vcol(FFN_CONV),
            gcol(1), vcol(1),
            pl.BlockSpec((tf, d), lambda b, i, c: (c, 0)),
            pl.BlockSpec((1, N_MOD, d), lambda b, i, c: (b, 0, 0)),
            _const_spec((1, d)),
        ],
        out_specs=any_spec,
        out_shape=jax.ShapeDtypeStruct((bsz, n, d), F32),
        scratch_shapes=[
            pltpu.VMEM((2, rows, d), BF16),
            pltpu.VMEM((tm, d), F32),
            pltpu.VMEM((tm, d), F32),
            pltpu.SemaphoreType.DMA((2,)),
            pltpu.SemaphoreType.DMA((1,)),
            pltpu.SemaphoreType.DMA((1,)),
        ],
        compiler_params=_cparams(("arbitrary", "arbitrary", "arbitrary"), FFN_VMEM_LIMIT_BYTES),
        name="conv_ffn",
    )(h2, x1, w_up, w_up, conv_w, conv_w, conv_b2, conv_b2, w_down, mod3, g_post.reshape(1, d))


def _grid_pos_tables(n_tokens, d_model):
    quarter = d_model // 4
    freqs = POS_BASE ** (-jnp.arange(quarter, dtype=F32) / quarter)

    def enc(count):
        ang = jnp.arange(count, dtype=F32)[:, None] * freqs[None, :]
        return jnp.concatenate([jnp.sin(ang), jnp.cos(ang)], axis=-1)

    return enc(n_tokens // GRID_W), enc(GRID_W)


def kernel(x, c, ctx, c_ctx, w_ada, b_ada, g_mix_pre, g_mix_post, g_ffn_pre, g_ffn_post,
           w_in, conv_lru_w, conv_lru_b, w_rec_gate, b_rec_gate, w_in_gate, b_in_gate,
           lru_lambda, w_fourier, b_fourier, w_out, w_up, conv_ffn_w, conv_ffn_b, w_down):
    bsz, n, d = x.shape
    d_fourier = w_fourier.shape[1] * w_fourier.shape[2]
    d_lru = conv_lru_w.shape[2]
    assert w_ada.shape[0] == 1, "single layer: the context stream is only read, never updated"
    assert bsz + 1 <= MOD_ROWS and d_lru // N_LRU_HEADS == LANES
    l = 0
    pe = _grid_pos_tables(n, d)

    cond = jnp.zeros((MOD_ROWS, d), F32).at[:bsz].set(c).at[bsz].set(c_ctx)
    mod3 = _modulation(cond, w_ada[l], b_ada[l]).reshape(MOD_ROWS, N_MOD, d)

    w_in_b = w_in[l].astype(BF16)
    x_cols = (d_fourier, d_fourier + d_lru)
    n_ctx = ctx.shape[1]
    (uxc,) = _in_proj(ctx.reshape(1, bsz * n_ctx, d), None, mod3, lambda b: bsz, g_mix_pre[l],
                      w_in_b[:, x_cols[0]:x_cols[1]], ((0, d_lru),), (F32,), IN_TM, "in_proj_ctx")
    uxc = uxc.reshape(bsz, n_ctx, d_lru)
    uf, ux, ug = _in_proj(x, pe, mod3, lambda b: b, g_mix_pre[l], w_in_b,
                          ((0, d_fourier), x_cols, (x_cols[1], x_cols[1] + d_lru)),
                          (BF16, F32, F32), IN_TM, "in_proj")

    wg = (0.5 * jnp.concatenate([w_rec_gate[l, 0], w_in_gate[l, 0], w_rec_gate[l, 1], w_in_gate[l, 1]],
                                axis=-1)).astype(BF16)
    bg = 0.5 * jnp.stack([b_rec_gate[l, 0], b_in_gate[l, 0], b_rec_gate[l, 1], b_in_gate[l, 1]])
    y_lru = _rg_lru(ux, uxc, ug, conv_lru_w[l], conv_lru_b[l], wg, bg, lru_lambda[l])
    y_fourier = _fourier(uf, w_fourier[l], b_fourier[l])

    x1, h2 = _out_proj(y_fourier, y_lru, x, pe, mod3, g_mix_post[l], g_ffn_pre[l], w_out[l].astype(BF16))
    return _conv_ffn(h2, x1, mod3, g_ffn_post[l], w_up[l], conv_ffn_w[l], conv_ffn_b[l], w_down[l])
```

```python
import functools
import itertools
import math

import numpy as np
import jax
import jax.numpy as jnp
from jax import lax
from jax.experimental import pallas as pl
from jax.experimental.pallas import tpu as pltpu

F32 = jnp.float32
BF16 = jnp.bfloat16

EPS = 1e-6
POS_BASE = 10000.0
GRID_W = 64
N_MOD = 6
N_FOURIER_GROUPS = 4
N_LRU_HEADS = 8
LRU_CONV = 4
LRU_C = 8.0
FFN_CONV = 3

LANES = 128
SUBLANES = 8
VMEM_BYTES = 64 * 1024 * 1024
VMEM_LIMIT_BYTES = 60000 * 1024
BIG_VMEM_LIMIT_BYTES = VMEM_BYTES - 2 * 1024 * 1024

MOD_ROWS = 16
MOD_TN = 2048
IN_TM = 512
IN_SUB_TILES = 2
OUT_SUB_ROWS = (256, 256)
LRU_CB = 512
LRU_ROW_CHUNK = 512
SCAN_UNROLL = 4
GATE_BIAS_ROWS = 2
FFN_TM = 1024
FFN_TF = 512
FFN_HALO = 16
FFN_EPILOGUE_ROWS = 256


def _cparams(sem, vmem_limit_bytes=VMEM_LIMIT_BYTES):
    return pltpu.CompilerParams(dimension_semantics=sem, vmem_limit_bytes=vmem_limit_bytes)


def _const_spec(shape):
    nd = len(shape)
    return pl.BlockSpec(shape, lambda *_: (0,) * nd, pipeline_mode=pl.Buffered(1))


def _mod_kernel(c_ref, w_ref, b_ref, o_ref):
    cv = c_ref[...]
    s = (cv * jax.nn.sigmoid(cv)).astype(BF16)
    o_ref[...] = jnp.dot(s, w_ref[...].astype(BF16), preferred_element_type=F32) + b_ref[...]


def _modulation(cond, w_ada, b_ada):
    d = cond.shape[1]
    n = w_ada.shape[1]
    tn = MOD_TN
    return pl.pallas_call(
        _mod_kernel,
        grid=(n // tn,),
        in_specs=[
            pl.BlockSpec((MOD_ROWS, d), lambda j: (0, 0)),
            pl.BlockSpec((d, tn), lambda j: (0, j)),
            pl.BlockSpec((1, tn), lambda j: (0, j)),
        ],
        out_specs=pl.BlockSpec((MOD_ROWS, tn), lambda j: (0, j)),
        out_shape=jax.ShapeDtypeStruct((MOD_ROWS, n), F32),
        compiler_params=_cparams(("parallel",)),
        name="modulation",
    )(cond, w_ada, b_ada.reshape(1, n))


def _rms_norm(v, g):
    ms = jnp.mean(v * v, axis=-1, keepdims=True)
    return v * lax.rsqrt(ms + EPS) * g


def _tile_pos_embed(prow_ref, pcol_ref):
    rows = jnp.concatenate([jnp.broadcast_to(prow_ref[r:r + 1, :], pcol_ref.shape)
                            for r in range(prow_ref.shape[0])], axis=0)
    cols = jnp.concatenate([pcol_ref[...]] * prow_ref.shape[0], axis=0)
    return jnp.concatenate([rows, cols], axis=1)


def _in_proj_kernel(*refs, has_pe, col_splits):
    if has_pe:
        x_ref, prow_ref, pcol_ref, mod_ref, g_ref, w_ref = refs[:6]
        out_refs = refs[6:]
        pe = _tile_pos_embed(prow_ref, pcol_ref)
    else:
        x_ref, mod_ref, g_ref, w_ref = refs[:4]
        out_refs = refs[4:]
        pe = None
    n_rows = x_ref.shape[1]
    sub = n_rows // IN_SUB_TILES
    gain = g_ref[...] * (1.0 + mod_ref[0, 1:2, :])
    for r0 in range(0, n_rows, sub):
        rs = slice(r0, r0 + sub)
        xp = x_ref[0, rs, :] if pe is None else x_ref[0, rs, :] + pe[rs, :]
        h = _rms_norm(xp, gain) + mod_ref[0, 0:1, :]
        hb = h.astype(BF16)
        for o_ref, (c0, c1) in zip(out_refs, col_splits):
            o_ref[0, rs, :] = jnp.dot(hb, w_ref[:, c0:c1], preferred_element_type=F32).astype(o_ref.dtype)


def _pos_embed_specs(pe, tm):
    prow, pcol = pe
    assert tm % GRID_W == 0
    return [pl.BlockSpec((tm // GRID_W, prow.shape[1]), lambda i, b: (i, 0)), _const_spec(pcol.shape)]


def _in_proj(x, pe, mod3, mod_row, g, w, w_cols, col_splits, out_dtypes, tm, name):
    bsz, n, d = x.shape
    w_width = w_cols[1] - w_cols[0]
    assert w_cols[0] % w_width == 0
    has_pe = pe is not None
    in_specs = [pl.BlockSpec((1, tm, d), lambda i, b: (b, i, 0))]
    args = [x]
    if has_pe:
        in_specs += _pos_embed_specs(pe, tm)
        args += list(pe)
    in_specs += [
        pl.BlockSpec((1, N_MOD, d), lambda i, b: (mod_row(b), 0, 0)),
        _const_spec((1, d)),
        pl.BlockSpec((w.shape[0], w_width), lambda i, b: (0, w_cols[0] // w_width), pipeline_mode=pl.Buffered(1)),
    ]
    args += [mod3, g.reshape(1, d), w]
    out_specs = [pl.BlockSpec((1, tm, c1 - c0), lambda i, b: (b, i, 0)) for c0, c1 in col_splits]
    out_shape = [jax.ShapeDtypeStruct((bsz, n, c1 - c0), dt) for (c0, c1), dt in zip(col_splits, out_dtypes)]
    return pl.pallas_call(
        functools.partial(_in_proj_kernel, has_pe=has_pe, col_splits=col_splits),
        grid=(n // tm, bsz),
        in_specs=in_specs,
        out_specs=out_specs,
        out_shape=out_shape,
        compiler_params=_cparams(("parallel", "parallel")),
        name=name,
    )(*args)


def _scan_pitch(n):
    m = -(-n // SUBLANES)
    while m % 8 != 4:
        m += 1
    return m


def _depthwise_conv_rows(x, w_ref, b_ref, left):
    n = x.shape[0]
    k_w = w_ref.shape[0]
    row = lax.broadcasted_iota(jnp.int32, x.shape, 0)
    y = b_ref[...] + jnp.zeros_like(x)
    for k in range(k_w):
        off = k - left
        if off == 0:
            xs = x
        elif off < 0:
            xs = jnp.where(row >= -off, pltpu.roll(x, -off, 0), 0.0)
        else:
            xs = jnp.where(row < n - off, pltpu.roll(x, n - off, 0), 0.0)
        y = y + xs * w_ref[k:k + 1, :]
    return y


def _log_sigmoid(z):
    return jnp.minimum(z, 0.0) - jnp.log1p(jnp.exp(-jnp.abs(z)))


def _lru_gates(xc_ref, n, row_chunk, wg_ref, quarter_c_ls, a_ref, b_ref, n_slabs):
    bias_cols = (lax.broadcasted_iota(jnp.int32, (row_chunk, LANES), 1) < GATE_BIAS_ROWS).astype(BF16)
    for r0 in range(0, n, row_chunk):
        for s in range(n_slabs):
            lanes = slice(s * LANES, (s + 1) * LANES)
            xh = xc_ref[r0:r0 + row_chunk, lanes]
            lhs = jnp.concatenate([xh.astype(BF16), bias_cols], axis=1)
            g = jnp.dot(lhs, wg_ref[s], preferred_element_type=F32)
            for d in range(2):
                tr = jnp.tanh(g[:, (2 * d) * LANES:(2 * d + 1) * LANES])
                ti = jnp.tanh(g[:, (2 * d + 1) * LANES:(2 * d + 2) * LANES])
                hl = -quarter_c_ls[d:d + 1, lanes]
                t = jnp.tanh(hl + hl * tr)
                q = pl.reciprocal(1.0 + t, full_range=False)
                root = jnp.where(t > 0.0, t * lax.rsqrt(t), 0.0)
                a_ref[d, s, r0:r0 + row_chunk, :] = (1.0 - t) * q
                b_ref[d, s, r0:r0 + row_chunk, :] = (root * q) * ((1.0 + ti) * xh)


def _lru_scan(a_ref, b_ref, n, h0, n_slabs, h_ref=None):
    m = _scan_pitch(n)
    pairs = [(d, s) for d in range(2) for s in range(n_slabs)]
    sub = lax.broadcasted_iota(jnp.int32, (SUBLANES, LANES), 0)

    def rows_of(d, j):
        return pl.ds(j if d == 0 else m - 1 - j, SUBLANES, stride=m)

    def summary_body(j, carry):
        out = []
        for (d, s), (h, p) in zip(pairs, carry):
            a = a_ref[d, s, rows_of(d, j), :]
            out.append((a * h + b_ref[d, s, rows_of(d, j), :], a * p))
        return tuple(out)

    init = tuple((jnp.zeros((SUBLANES, LANES), F32), jnp.ones((SUBLANES, LANES), F32)) for _ in pairs)
    local = lax.fori_loop(0, m, summary_body, init, unroll=SCAN_UNROLL)

    carries, finals = [], []
    for (d, s), (hfin, ptot) in zip(pairs, local):
        first, last, shift = (0, SUBLANES - 1, 1) if d == 0 else (SUBLANES - 1, 0, SUBLANES - 1)
        hin = jnp.where(sub == first, jnp.broadcast_to(h0[d][s], (SUBLANES, LANES)), 0.0)
        for step in range(1, SUBLANES):
            k = first + step if d == 0 else first - step
            nxt = pltpu.roll(hfin + ptot * hin, shift, 0)
            hin = jnp.where(sub == k, nxt, hin)
        carries.append(hin)
        finals.append((hfin + ptot * hin)[last:last + 1, :])

    if h_ref is not None:
        def state_body(j, carry):
            out = []
            for (d, s), h in zip(pairs, carry):
                h = a_ref[d, s, rows_of(d, j), :] * h + b_ref[d, s, rows_of(d, j), :]
                h_ref[d, s, rows_of(d, j), :] = h
                out.append(h)
            return tuple(out)

        lax.fori_loop(0, m, state_body, tuple(carries), unroll=SCAN_UNROLL)

    return [[finals[d * n_slabs + s] for s in range(n_slabs)] for d in range(2)]


def _lru_kernel(ux_ref, uxc_ref, ug_ref, cw_ref, cb_ref, wg_ref, lam_ref, o_ref,
                xc_s, xcc_s, a_s, b_s, ac_s, bc_s, h_s):
    n = ux_ref.shape[1]
    n_ctx = uxc_ref.shape[1]
    n_slabs = ux_ref.shape[2] // LANES
    ls = (0.25 * LRU_C) * _log_sigmoid(lam_ref[...])

    for ref, rows in ((a_s, n), (ac_s, n_ctx)):
        ref[:, :, rows:, :] = jnp.ones((2, n_slabs, ref.shape[2] - rows, LANES), F32)
    for ref, rows in ((b_s, n), (bc_s, n_ctx)):
        ref[:, :, rows:, :] = jnp.zeros((2, n_slabs, ref.shape[2] - rows, LANES), F32)

    left = LRU_CONV // 2
    for s in range(n_slabs):
        lanes = slice(s * LANES, (s + 1) * LANES)
        taps = (cw_ref.at[:, lanes], cb_ref.at[:, lanes], left)
        xcc_s[:, lanes] = _depthwise_conv_rows(uxc_ref[0, :, lanes], *taps)
        xc_s[:, lanes] = _depthwise_conv_rows(ux_ref[0, :, lanes], *taps)

    _lru_gates(xcc_s, n_ctx, n_ctx, wg_ref, ls, ac_s, bc_s, n_slabs)
    _lru_gates(xc_s, n, LRU_ROW_CHUNK, wg_ref, ls, a_s, b_s, n_slabs)

    zero = jnp.zeros((1, LANES), F32)
    h0 = _lru_scan(ac_s, bc_s, n_ctx, [[zero] * n_slabs] * 2, n_slabs)
    _lru_scan(a_s, b_s, n, h0, n_slabs, h_s)

    for r0 in range(0, n, LRU_ROW_CHUNK):
        rows = slice(r0, r0 + LRU_ROW_CHUNK)
        for s in range(n_slabs):
            lanes = slice(s * LANES, (s + 1) * LANES)
            hsum = h_s[0, s, rows, :] + h_s[1, s, rows, :]
            gate = jax.nn.gelu(ug_ref[0, rows, lanes], approximate=True)
            o_ref[0, rows, lanes] = (hsum * gate).astype(o_ref.dtype)


def _rg_lru(ux, uxc, ug, conv_w, conv_b, wg, lam):
    bsz, n, c = ux.shape
    n_ctx = uxc.shape[1]
    cb = LRU_CB
    n_slabs = cb // LANES
    lp = SUBLANES * _scan_pitch(n)
    cp = SUBLANES * _scan_pitch(n_ctx)
    seq = lambda rows: pl.BlockSpec((1, rows, cb), lambda b, j: (b, 0, j))
    par = lambda rows: pl.BlockSpec((rows, cb), lambda b, j: (0, j))
    return pl.pallas_call(
        _lru_kernel,
        grid=(bsz, c // cb),
        in_specs=[
            seq(n), seq(n_ctx), seq(n),
            par(conv_w.shape[0]), par(1),
            pl.BlockSpec((n_slabs,) + wg.shape[1:], lambda b, j: (j, 0, 0)),
            par(2),
        ],
        out_specs=seq(n),
        out_shape=jax.ShapeDtypeStruct((bsz, n, c), BF16),
        scratch_shapes=[
            pltpu.VMEM((n, cb), F32),
            pltpu.VMEM((n_ctx, cb), F32),
            pltpu.VMEM((2, n_slabs, lp, LANES), F32),
            pltpu.VMEM((2, n_slabs, lp, LANES), F32),
            pltpu.VMEM((2, n_slabs, cp, LANES), F32),
            pltpu.VMEM((2, n_slabs, cp, LANES), F32),
            pltpu.VMEM((2, n_slabs, lp, LANES), F32),
        ],
        compiler_params=_cparams(("parallel", "parallel")),
        name="rg_lru",
    )(ux, uxc, ug, conv_w, conv_b.reshape(1, c), wg, lam)


FFT_RADIX = 8
_FFT_BLOCK_OF = {0: (0, 0.0), 4: (4, 0.0), 2: (2, -1.0), 6: (2, 1.0), 1: (1, -1.0), 7: (1, 1.0), 3: (3, 1.0), 5: (3, -1.0)}


def _fourier_kernel(u_ref, m_ref, ccs_ref, wf_ref, bf_ref, o_ref, t_s, z_s, y_s):
    n = u_ref.shape[1]
    r = n // FFT_RADIX
    groups, gw, _ = wf_ref.shape
    c = math.sqrt(0.5)
    imag_blocks = (1, 2, 3)

    for g in range(groups):
        cols = slice(g * gw, (g + 1) * gw)
        x = [u_ref[0, p * r:(p + 1) * r, cols].astype(F32) for p in range(FFT_RADIX)]
        e0, e1, e2, e3 = x[0] + x[4], x[0] - x[4], x[2] + x[6], x[2] - x[6]
        o0, o1, o2, o3 = x[1] + x[5], x[1] - x[5], x[3] + x[7], x[3] - x[7]
        ee, oo = e0 + e2, o0 + o2
        p, m = c * (o1 - o3), c * (o1 + o3)
        t_re = {0: ee + oo, 4: ee - oo, 2: e0 - e2, 1: e1 + p, 3: e1 - p}
        t_im = {2: o0 - o2, 1: m + e3, 3: e3 - m}
        for blk in t_re:
            t_s[blk, 0, :, cols] = t_re[blk].astype(BF16)
            if blk in t_im:
                t_s[blk, 1, :, cols] = t_im[blk].astype(BF16)

        for k1 in range(FFT_RADIX):
            blk = _FFT_BLOCK_OF[k1][0]
            if blk in imag_blocks:
                z = jnp.dot(m_ref[k1], t_s[blk, :, :, cols].reshape(2 * r, gw), preferred_element_type=F32)
            else:
                z = jnp.dot(m_ref[k1, :, :r], t_s[blk, 0, :, cols], preferred_element_type=F32)
            z_s[k1, :, :, cols] = z.reshape(2, r, gw).astype(BF16)

        f = jnp.dot(z_s[:, 0, :, cols].reshape(n, gw), ccs_ref[0], preferred_element_type=F32)
        f = f + jnp.dot(z_s[:, 1, :, cols].reshape(n, gw), ccs_ref[1], preferred_element_type=F32)
        y = jnp.dot(f.astype(BF16), wf_ref[g], preferred_element_type=F32) + bf_ref[:, cols]
        for k1 in range(FFT_RADIX):
            for h in range(gw // LANES):
                slab = g * (gw // LANES) + h
                y_s[slab, pl.ds(k1, r, stride=FFT_RADIX), :] = y[k1 * r:(k1 + 1) * r, h * LANES:(h + 1) * LANES]

    for slab in range(y_s.shape[0]):
        o_ref[0, :, slab * LANES:(slab + 1) * LANES] = y_s[slab].astype(o_ref.dtype)


@functools.lru_cache(maxsize=None)
def _fourier_tables(n, gw):
    r = n // FFT_RADIX
    k = np.arange(FFT_RADIX)[:, None, None] + FFT_RADIX * np.arange(r)[None, :, None]
    ang = ((k * np.arange(r)[None, None, :]) % n) * (2.0 * math.pi / n)
    cn, sn = np.cos(ang) * n ** -0.5, np.sin(ang) * n ** -0.5
    sign = np.array([_FFT_BLOCK_OF[k1][1] for k1 in range(FFT_RADIX)])[:, None, None]
    m = np.concatenate([np.concatenate([cn, sign * sn], axis=2),
                        np.concatenate([-sn, sign * cn], axis=2)], axis=1)
    idx = np.arange(gw)
    ang_c = ((idx[:, None] * idx[None, :]) % gw) * (2.0 * math.pi / gw)
    ccs = np.stack([np.cos(ang_c), np.sin(ang_c)]) * gw ** -0.5
    return m.astype(np.float32), ccs.astype(np.float32)


def _fourier(uf, w_f, b_f):
    bsz, n, c = uf.shape
    groups, gw, _ = w_f.shape
    assert n % FFT_RADIX == 0 and gw % LANES == 0
    m, ccs = (jnp.asarray(t).astype(BF16) for t in _fourier_tables(n, gw))
    return pl.pallas_call(
        _fourier_kernel,
        grid=(bsz,),
        in_specs=[
            pl.BlockSpec((1, n, c), lambda b: (b, 0, 0)),
            _const_spec(m.shape),
            _const_spec(ccs.shape),
            _const_spec(w_f.shape),
            _const_spec((1, c)),
        ],
        out_specs=pl.BlockSpec((1, n, c), lambda b: (b, 0, 0)),
        out_shape=jax.ShapeDtypeStruct((bsz, n, c), BF16),
        scratch_shapes=[
            pltpu.VMEM((FFT_RADIX // 2 + 1, 2, n // FFT_RADIX, c), BF16),
            pltpu.VMEM((FFT_RADIX, 2, n // FFT_RADIX, c), BF16),
            pltpu.VMEM((c // LANES, n, LANES), F32),
        ],
        compiler_params=_cparams(("parallel",)),
        name="fourier",
    )(uf, m, ccs, w_f.astype(BF16), b_f.reshape(1, c))


def _out_proj_kernel(yf_ref, yl_ref, x_ref, prow_ref, pcol_ref, mod_ref, gpost_ref, gpre_ref, w_ref,
                     x1_ref, h2_ref):
    k = yf_ref.shape[2]
    pe = _tile_pos_embed(prow_ref, pcol_ref)
    assert sum(OUT_SUB_ROWS) == x_ref.shape[1]
    gain_post = mod_ref[0, 2:3, :] * gpost_ref[...]
    gain_pre = gpre_ref[...] * (1.0 + mod_ref[0, 4:5, :])
    for r0, n_rows in zip(itertools.accumulate((0,) + OUT_SUB_ROWS), OUT_SUB_ROWS):
        rs = slice(r0, r0 + n_rows)
        y = jnp.dot(yf_ref[0, rs, :], w_ref[:k, :], preferred_element_type=F32)
        y = y + jnp.dot(yl_ref[0, rs, :], w_ref[k:, :], preferred_element_type=F32)
        x1 = (x_ref[0, rs, :] + pe[rs, :]) + _rms_norm(y, gain_post)
        x1_ref[0, rs, :] = x1
        h2 = _rms_norm(x1, gain_pre) + mod_ref[0, 3:4, :]
        h2_ref[0, rs, :] = h2.astype(h2_ref.dtype)


def _out_proj(yf, yl, x, pe, mod3, g_post, g_pre, w_out):
    bsz, n, d = x.shape
    k = yf.shape[2]
    tm = sum(OUT_SUB_ROWS)
    act = pl.BlockSpec((1, tm, k), lambda i, b: (b, i, 0))
    row = pl.BlockSpec((1, tm, d), lambda i, b: (b, i, 0))
    return pl.pallas_call(
        _out_proj_kernel,
        grid=(n // tm, bsz),
        in_specs=[
            act, act, row,
            *_pos_embed_specs(pe, tm),
            pl.BlockSpec((1, N_MOD, d), lambda i, b: (b, 0, 0)),
            _const_spec((1, d)), _const_spec((1, d)),
            _const_spec(w_out.shape),
        ],
        out_specs=[row, row],
        out_shape=[jax.ShapeDtypeStruct((bsz, n, d), F32), jax.ShapeDtypeStruct((bsz, n, d), BF16)],
        compiler_params=_cparams(("parallel", "parallel")),
        name="out_proj",
    )(yf, yl, x, *pe, mod3, g_post.reshape(1, d), g_pre.reshape(1, d), w_out)


def _ffn_kernel(h_hbm, x1_hbm, wg_ref, wv_ref, cwg_ref, cwv_ref, cbg_ref, cbv_ref, wd_ref, mod_ref, gpost_ref,
                o_hbm, lhs_s, acc_s, xo_s, lhs_sem, x1_sem, out_sem, *, n_t, n_f):
    b, i, c = pl.program_id(0), pl.program_id(1), pl.program_id(2)
    tm = acc_s.shape[0]
    rows = lhs_s.shape[1]
    halo = (rows - tm) // 2
    t = b * n_t + i
    slot = t % 2
    last_tile = pl.num_programs(0) * n_t - 1

    def lhs_dma(bb, ii, sl, start):
        def run(src_row0, dst_row0, n_rows):
            cp = pltpu.make_async_copy(h_hbm.at[bb, pl.ds(src_row0, n_rows)],
                                       lhs_s.at[sl, pl.ds(dst_row0, n_rows)], lhs_sem.at[sl])
            if start:
                cp.start()
            else:
                cp.wait()

        pl.when(ii == 0)(lambda: run(0, halo, tm + halo))
        pl.when(ii == n_t - 1)(lambda: run((n_t - 1) * tm - halo, 0, tm + halo))
        if n_t > 2:
            pl.when(jnp.logical_and(ii > 0, ii < n_t - 1))(
                lambda: run(pl.multiple_of(ii * tm - halo, halo), 0, rows))

    def x1_copy():
        return pltpu.make_async_copy(x1_hbm.at[b, pl.ds(pl.multiple_of(i * tm, tm), tm)], xo_s, x1_sem.at[0])

    def out_copy(bb, ii):
        return pltpu.make_async_copy(xo_s, o_hbm.at[bb, pl.ds(pl.multiple_of(ii * tm, tm), tm)], out_sem.at[0])

    def conv_branch(w_ref, cw_ref, cb_ref):
        up = jnp.dot(lhs_s[slot], w_ref[...].astype(BF16), preferred_element_type=F32)
        left = FFN_CONV // 2
        y = cb_ref[...] + jnp.zeros((tm, up.shape[1]), F32)
        for k in range(FFN_CONV):
            off = k - left
            shifted = up if off == 0 else pltpu.roll(up, (-off) % rows, 0)
            y = y + shifted[halo:halo + tm, :] * cw_ref[k:k + 1, :]
        return y

    @pl.when(c == 0)
    def _():
        pl.when(t == 0)(lambda: lhs_dma(b, i, slot, True))
        lhs_dma(b, i, slot, False)
        zeros = jnp.zeros((halo, lhs_s.shape[2]), lhs_s.dtype)

        @pl.when(i == 0)
        def _():
            lhs_s[slot, 0:halo, :] = zeros

        @pl.when(i == n_t - 1)
        def _():
            lhs_s[slot, halo + tm:, :] = zeros

    @pl.when(c == 1)
    def _():
        nxt = i + 1 < n_t
        pl.when(t < last_tile)(
            lambda: lhs_dma(jnp.where(nxt, b, b + 1), jnp.where(nxt, i + 1, 0), 1 - slot, True))
        prv = i > 0
        pl.when(t > 0)(lambda: out_copy(jnp.where(prv, b, b - 1), jnp.where(prv, i - 1, n_t - 1)).wait())
        x1_copy().start()

    def gated_activation():
        g = conv_branch(wg_ref, cwg_ref, cbg_ref)
        v = conv_branch(wv_ref, cwv_ref, cbv_ref)
        return (jax.nn.gelu(g, approximate=True) * v).astype(BF16)

    @pl.when(c == 0)
    def _():
        acc_s[...] = jnp.dot(gated_activation(), wd_ref[...].astype(BF16), preferred_element_type=F32)

    @pl.when(jnp.logical_and(c > 0, c < n_f - 1))
    def _():
        acc_s[...] += jnp.dot(gated_activation(), wd_ref[...].astype(BF16), preferred_element_type=F32)

    @pl.when(c == n_f - 1)
    def _():
        act = gated_activation()
        wd = wd_ref[...].astype(BF16)
        x1_copy().wait()
        gain = mod_ref[0, 5:6, :] * gpost_ref[...]
        for r0 in range(0, tm, FFN_EPILOGUE_ROWS):
            rs = slice(r0, r0 + FFN_EPILOGUE_ROWS)
            y = acc_s[rs, :] + jnp.dot(act[rs, :], wd, preferred_element_type=F32)
            xo_s[rs, :] = xo_s[rs, :] + _rms_norm(y, gain)
        out_copy(b, i).start()
        pl.when(t == last_tile)(lambda: out_copy(b, i).wait())


def _conv_ffn(h2, x1, mod3, g_post, w_up, conv_w, conv_b, w_down):
    bsz, n, d = h2.shape
    d_ff = w_down.shape[0]
    tm, tf, halo = FFN_TM, FFN_TF, FFN_HALO
    nt, nf = n // tm, d_ff // tf
    assert nt >= 2 and nf >= 2 and n % tm == 0 and d_ff % tf == 0 and tm % FFN_EPILOGUE_ROWS == 0
    gcol = lambda r: pl.BlockSpec((r, tf), lambda b, i, c: (0, c))
    vcol = lambda r: pl.BlockSpec((r, tf), lambda b, i, c: (0, c + nf))
    any_spec = pl.BlockSpec(memory_space=pl.ANY)
    conv_b2 = conv_b.reshape(1, 2 * d_ff)
    rows = tm + 2 * halo
    return pl.pallas_call(
        functools.partial(_ffn_kernel, n_t=nt, n_f=nf),
        grid=(bsz, nt, nf),
        in_specs=[
            any_spec, any_spec,
            gcol(d), vcol(d),
            gcol(FFN_CONV), vcol(FFN_CONV),
            gcol(1), vcol(1),
            pl.BlockSpec((tf, d), lambda b, i, c: (c, 0)),
            pl.BlockSpec((1, N_MOD, d), lambda b, i, c: (b, 0, 0)),
            _const_spec((1, d)),
        ],
        out_specs=any_spec,
        out_shape=jax.ShapeDtypeStruct((bsz, n, d), F32),
        scratch_shapes=[
            pltpu.VMEM((2, rows, d), BF16),
            pltpu.VMEM((tm, d), F32),
            pltpu.VMEM((tm, d), F32),
            pltpu.SemaphoreType.DMA((2,)),
            pltpu.SemaphoreType.DMA((1,)),
            pltpu.SemaphoreType.DMA((1,)),
        ],
        compiler_params=_cparams(("arbitrary", "arbitrary", "arbitrary"), BIG_VMEM_LIMIT_BYTES),
        name="conv_ffn",
    )(h2, x1, w_up, w_up, conv_w, conv_w, conv_b2, conv_b2, w_down, mod3, g_post.reshape(1, d))


def _gate_weights(w_rec, b_rec, w_in, b_in):
    heads, hd, _ = w_rec.shape[1:]
    w = 0.5 * jnp.concatenate([w_rec[0], w_in[0], w_rec[1], w_in[1]], axis=-1)
    b = 0.5 * jnp.concatenate([v.reshape(heads, hd) for v in (b_rec[0], b_in[0], b_rec[1], b_in[1])], axis=-1)
    b_hi = b.astype(BF16)
    b_lo = (b - b_hi.astype(F32)).astype(BF16)
    pad = jnp.zeros((heads, hd - GATE_BIAS_ROWS, w.shape[-1]), BF16)
    return jnp.concatenate([w.astype(BF16), b_hi[:, None, :], b_lo[:, None, :], pad], axis=1)


def _grid_pos_tables(n_tokens, d_model):
    quarter = d_model // 4
    freqs = POS_BASE ** (-jnp.arange(quarter, dtype=F32) / quarter)

    def enc(count):
        ang = jnp.arange(count, dtype=F32)[:, None] * freqs[None, :]
        return jnp.concatenate([jnp.sin(ang), jnp.cos(ang)], axis=-1)

    return enc(n_tokens // GRID_W), enc(GRID_W)


def kernel(x, c, ctx, c_ctx, w_ada, b_ada, g_mix_pre, g_mix_post, g_ffn_pre, g_ffn_post,
           w_in, conv_lru_w, conv_lru_b, w_rec_gate, b_rec_gate, w_in_gate, b_in_gate,
           lru_lambda, w_fourier, b_fourier, w_out, w_up, conv_ffn_w, conv_ffn_b, w_down):
    bsz, n, d = x.shape
    d_fourier = w_fourier.shape[1] * w_fourier.shape[2]
    d_lru = conv_lru_w.shape[2]
    assert w_ada.shape[0] == 1, "single layer: the context stream is only read, never updated"
    assert bsz + 1 <= MOD_ROWS and d_lru // N_LRU_HEADS == LANES
    l = 0
    pe = _grid_pos_tables(n, d)

    cond = jnp.zeros((MOD_ROWS, d), F32).at[:bsz].set(c).at[bsz].set(c_ctx)
    mod3 = _modulation(cond, w_ada[l], b_ada[l]).reshape(MOD_ROWS, N_MOD, d)

    w_in_b = w_in[l].astype(BF16)
    x_cols = (d_fourier, d_fourier + d_lru)
    n_ctx = ctx.shape[1]
    (uxc,) = _in_proj(ctx.reshape(1, bsz * n_ctx, d), None, mod3, lambda b: bsz, g_mix_pre[l],
                      w_in_b, x_cols, ((0, d_lru),), (F32,), IN_TM, "in_proj_ctx")
    uxc = uxc.reshape(bsz, n_ctx, d_lru)
    uf, ux, ug = _in_proj(x, pe, mod3, lambda b: b, g_mix_pre[l], w_in_b, (0, w_in_b.shape[1]),
                          ((0, d_fourier), x_cols, (x_cols[1], x_cols[1] + d_lru)),
                          (BF16, F32, F32), IN_TM, "in_proj")

    y_lru = _rg_lru(ux, uxc, ug, conv_lru_w[l], conv_lru_b[l],
                    _gate_weights(w_rec_gate[l], b_rec_gate[l], w_in_gate[l], b_in_gate[l]), lru_lambda[l])
    y_fourier = _fourier(uf, w_fourier[l], b_fourier[l])

    x1, h2 = _out_proj(y_fourier, y_lru, x, pe, mod3, g_mix_post[l], g_ffn_pre[l], w_out[l].astype(BF16))
    return _conv_ffn(h2, x1, mod3, g_ffn_post[l], w_up[l], conv_ffn_w[l], conv_ffn_b[l], w_down[l])
```

```python
import functools
import itertools
import math

import numpy as np
import jax
import jax.numpy as jnp
from jax import lax
from jax.experimental import pallas as pl
from jax.experimental.pallas import tpu as pltpu

F32 = jnp.float32
BF16 = jnp.bfloat16

EPS = 1e-6
POS_BASE = 10000.0
GRID_W = 64
N_MOD = 6
N_FOURIER_GROUPS = 4
N_LRU_HEADS = 8
LRU_CONV = 4
LRU_C = 8.0
FFN_CONV = 3

LANES = 128
SUBLANES = 8
VMEM_BYTES = 64 * 1024 * 1024
VMEM_LIMIT_BYTES = 60000 * 1024
BIG_VMEM_LIMIT_BYTES = VMEM_BYTES - 2 * 1024 * 1024

MOD_ROWS = 16
MOD_TN = 1024
IN_TM = 1024
IN_SUB_TILES = 4
OUT_SUB_ROWS = (256, 256)
LRU_CB = 512
LRU_ROW_CHUNK = 512
SCAN_UNROLL = 4
GATE_BIAS_ROWS = 2
FFN_TM = 1024
FFN_TF = 512
FFN_HALO = 16
FFN_EPILOGUE_ROWS = 256


def _cparams(sem, vmem_limit_bytes=VMEM_LIMIT_BYTES):
    return pltpu.CompilerParams(dimension_semantics=sem, vmem_limit_bytes=vmem_limit_bytes)


def _const_spec(shape):
    nd = len(shape)
    return pl.BlockSpec(shape, lambda *_: (0,) * nd, pipeline_mode=pl.Buffered(1))


def _mod_kernel(c_ref, w_ref, b_ref, o_ref):
    cv = c_ref[...]
    s = (cv * jax.nn.sigmoid(cv)).astype(BF16)
    o_ref[...] = jnp.dot(s, w_ref[...].astype(BF16), preferred_element_type=F32) + b_ref[...]


def _modulation(cond, w_ada, b_ada):
    d = cond.shape[1]
    n = w_ada.shape[1]
    tn = MOD_TN
    return pl.pallas_call(
        _mod_kernel,
        grid=(n // tn,),
        in_specs=[
            pl.BlockSpec((MOD_ROWS, d), lambda j: (0, 0)),
            pl.BlockSpec((d, tn), lambda j: (0, j)),
            pl.BlockSpec((1, tn), lambda j: (0, j)),
        ],
        out_specs=pl.BlockSpec((MOD_ROWS, tn), lambda j: (0, j)),
        out_shape=jax.ShapeDtypeStruct((MOD_ROWS, n), F32),
        compiler_params=_cparams(("parallel",)),
        name="modulation",
    )(cond, w_ada, b_ada.reshape(1, n))


def _rms_norm(v, g):
    ms = jnp.mean(v * v, axis=-1, keepdims=True)
    return v * lax.rsqrt(ms + EPS) * g


def _tile_pos_embed(prow_ref, pcol_ref):
    rows = jnp.concatenate([jnp.broadcast_to(prow_ref[r:r + 1, :], pcol_ref.shape)
                            for r in range(prow_ref.shape[0])], axis=0)
    cols = jnp.concatenate([pcol_ref[...]] * prow_ref.shape[0], axis=0)
    return jnp.concatenate([rows, cols], axis=1)


def _in_proj_kernel(*refs, has_pe, col_splits):
    if has_pe:
        x_ref, prow_ref, pcol_ref, mod_ref, g_ref, w_ref = refs[:6]
        out_refs = refs[6:]
        pe = _tile_pos_embed(prow_ref, pcol_ref)
    else:
        x_ref, mod_ref, g_ref, w_ref = refs[:4]
        out_refs = refs[4:]
        pe = None
    n_rows = x_ref.shape[1]
    sub = n_rows // IN_SUB_TILES
    gain = g_ref[...] * (1.0 + mod_ref[0, 1:2, :])
    for r0 in range(0, n_rows, sub):
        rs = slice(r0, r0 + sub)
        xp = x_ref[0, rs, :] if pe is None else x_ref[0, rs, :] + pe[rs, :]
        h = _rms_norm(xp, gain) + mod_ref[0, 0:1, :]
        hb = h.astype(BF16)
        for o_ref, (c0, c1) in zip(out_refs, col_splits):
            o_ref[0, rs, :] = jnp.dot(hb, w_ref[:, c0:c1], preferred_element_type=F32).astype(o_ref.dtype)


def _pos_embed_specs(pe, tm):
    prow, pcol = pe
    assert tm % GRID_W == 0
    return [pl.BlockSpec((tm // GRID_W, prow.shape[1]), lambda i, b: (i, 0)), _const_spec(pcol.shape)]


def _in_proj(x, pe, mod3, mod_row, g, w, w_cols, col_splits, out_dtypes, tm, name):
    bsz, n, d = x.shape
    w_width = w_cols[1] - w_cols[0]
    assert w_cols[0] % w_width == 0
    has_pe = pe is not None
    in_specs = [pl.BlockSpec((1, tm, d), lambda i, b: (b, i, 0))]
    args = [x]
    if has_pe:
        in_specs += _pos_embed_specs(pe, tm)
        args += list(pe)
    in_specs += [
        pl.BlockSpec((1, N_MOD, d), lambda i, b: (mod_row(b), 0, 0)),
        _const_spec((1, d)),
        pl.BlockSpec((w.shape[0], w_width), lambda i, b: (0, w_cols[0] // w_width), pipeline_mode=pl.Buffered(1)),
    ]
    args += [mod3, g.reshape(1, d), w]
    out_specs = [pl.BlockSpec((1, tm, c1 - c0), lambda i, b: (b, i, 0)) for c0, c1 in col_splits]
    out_shape = [jax.ShapeDtypeStruct((bsz, n, c1 - c0), dt) for (c0, c1), dt in zip(col_splits, out_dtypes)]
    return pl.pallas_call(
        functools.partial(_in_proj_kernel, has_pe=has_pe, col_splits=col_splits),
        grid=(n // tm, bsz),
        in_specs=in_specs,
        out_specs=out_specs,
        out_shape=out_shape,
        compiler_params=_cparams(("parallel", "parallel")),
        name=name,
    )(*args)


def _scan_pitch(n):
    m = -(-n // SUBLANES)
    while m % 8 != 4:
        m += 1
    return m


def _depthwise_conv_rows(x, w_ref, b_ref, left):
    n = x.shape[0]
    k_w = w_ref.shape[0]
    row = lax.broadcasted_iota(jnp.int32, x.shape, 0)
    y = b_ref[...] + jnp.zeros_like(x)
    for k in range(k_w):
        off = k - left
        if off == 0:
            xs = x
        elif off < 0:
            xs = jnp.where(row >= -off, pltpu.roll(x, -off, 0), 0.0)
        else:
            xs = jnp.where(row < n - off, pltpu.roll(x, n - off, 0), 0.0)
        y = y + xs * w_ref[k:k + 1, :]
    return y


def _log_sigmoid(z):
    return jnp.minimum(z, 0.0) - jnp.log1p(jnp.exp(-jnp.abs(z)))


def _lru_gates(xc_ref, n, row_chunk, wg_ref, quarter_c_ls, a_ref, b_ref, n_slabs):
    bias_cols = (lax.broadcasted_iota(jnp.int32, (row_chunk, LANES), 1) < GATE_BIAS_ROWS).astype(BF16)
    for r0 in range(0, n, row_chunk):
        for s in range(n_slabs):
            lanes = slice(s * LANES, (s + 1) * LANES)
            xh = xc_ref[r0:r0 + row_chunk, lanes]
            lhs = jnp.concatenate([xh.astype(BF16), bias_cols], axis=1)
            g = jnp.dot(lhs, wg_ref[s], preferred_element_type=F32)
            for d in range(2):
                tr = jnp.tanh(g[:, (2 * d) * LANES:(2 * d + 1) * LANES])
                ti = jnp.tanh(g[:, (2 * d + 1) * LANES:(2 * d + 2) * LANES])
                hl = -quarter_c_ls[d:d + 1, lanes]
                t = jnp.tanh(hl + hl * tr)
                q = pl.reciprocal(1.0 + t, full_range=False)
                root = jnp.where(t > 0.0, t * lax.rsqrt(t), 0.0)
                a_ref[d, s, r0:r0 + row_chunk, :] = (1.0 - t) * q
                b_ref[d, s, r0:r0 + row_chunk, :] = (root * q) * ((1.0 + ti) * xh)


def _lru_scan(a_ref, b_ref, n, h0, n_slabs, h_ref=None):
    m = _scan_pitch(n)
    pairs = [(d, s) for d in range(2) for s in range(n_slabs)]
    sub = lax.broadcasted_iota(jnp.int32, (SUBLANES, LANES), 0)

    def rows_of(d, j):
        return pl.ds(j if d == 0 else m - 1 - j, SUBLANES, stride=m)

    def summary_body(j, carry):
        out = []
        for (d, s), (h, p) in zip(pairs, carry):
            a = a_ref[d, s, rows_of(d, j), :]
            out.append((a * h + b_ref[d, s, rows_of(d, j), :], a * p))
        return tuple(out)

    init = tuple((jnp.zeros((SUBLANES, LANES), F32), jnp.ones((SUBLANES, LANES), F32)) for _ in pairs)
    local = lax.fori_loop(0, m, summary_body, init, unroll=SCAN_UNROLL)

    carries, finals = [], []
    for (d, s), (hfin, ptot) in zip(pairs, local):
        first, last, shift = (0, SUBLANES - 1, 1) if d == 0 else (SUBLANES - 1, 0, SUBLANES - 1)
        hin = jnp.where(sub == first, jnp.broadcast_to(h0[d][s], (SUBLANES, LANES)), 0.0)
        for step in range(1, SUBLANES):
            k = first + step if d == 0 else first - step
            nxt = pltpu.roll(hfin + ptot * hin, shift, 0)
            hin = jnp.where(sub == k, nxt, hin)
        carries.append(hin)
        finals.append((hfin + ptot * hin)[last:last + 1, :])

    if h_ref is not None:
        def state_body(j, carry):
            out = []
            for (d, s), h in zip(pairs, carry):
                h = a_ref[d, s, rows_of(d, j), :] * h + b_ref[d, s, rows_of(d, j), :]
                h_ref[d, s, rows_of(d, j), :] = h
                out.append(h)
            return tuple(out)

        lax.fori_loop(0, m, state_body, tuple(carries), unroll=SCAN_UNROLL)

    return [[finals[d * n_slabs + s] for s in range(n_slabs)] for d in range(2)]


def _lru_kernel(ux_ref, uxc_ref, ug_ref, cw_ref, cb_ref, wg_ref, lam_ref, o_ref,
                xc_s, xcc_s, a_s, b_s, ac_s, bc_s, h_s):
    n = ux_ref.shape[1]
    n_ctx = uxc_ref.shape[1]
    n_slabs = ux_ref.shape[2] // LANES
    ls = (0.25 * LRU_C) * _log_sigmoid(lam_ref[...])

    for ref, rows in ((a_s, n), (ac_s, n_ctx)):
        ref[:, :, rows:, :] = jnp.ones((2, n_slabs, ref.shape[2] - rows, LANES), F32)
    for ref, rows in ((b_s, n), (bc_s, n_ctx)):
        ref[:, :, rows:, :] = jnp.zeros((2, n_slabs, ref.shape[2] - rows, LANES), F32)

    left = LRU_CONV // 2
    for s in range(n_slabs):
        lanes = slice(s * LANES, (s + 1) * LANES)
        taps = (cw_ref.at[:, lanes], cb_ref.at[:, lanes], left)
        xcc_s[:, lanes] = _depthwise_conv_rows(uxc_ref[0, :, lanes], *taps)
        xc_s[:, lanes] = _depthwise_conv_rows(ux_ref[0, :, lanes], *taps)

    _lru_gates(xcc_s, n_ctx, n_ctx, wg_ref, ls, ac_s, bc_s, n_slabs)
    _lru_gates(xc_s, n, LRU_ROW_CHUNK, wg_ref, ls, a_s, b_s, n_slabs)

    zero = jnp.zeros((1, LANES), F32)
    h0 = _lru_scan(ac_s, bc_s, n_ctx, [[zero] * n_slabs] * 2, n_slabs)
    _lru_scan(a_s, b_s, n, h0, n_slabs, h_s)

    for r0 in range(0, n, LRU_ROW_CHUNK):
        rows = slice(r0, r0 + LRU_ROW_CHUNK)
        for s in range(n_slabs):
            lanes = slice(s * LANES, (s + 1) * LANES)
            hsum = h_s[0, s, rows, :] + h_s[1, s, rows, :]
            gate = jax.nn.gelu(ug_ref[0, rows, lanes], approximate=True)
            o_ref[0, rows, lanes] = (hsum * gate).astype(o_ref.dtype)


def _rg_lru(ux, uxc, ug, conv_w, conv_b, wg, lam):
    bsz, n, c = ux.shape
    n_ctx = uxc.shape[1]
    cb = LRU_CB
    n_slabs = cb // LANES
    lp = SUBLANES * _scan_pitch(n)
    cp = SUBLANES * _scan_pitch(n_ctx)
    seq = lambda rows: pl.BlockSpec((1, rows, cb), lambda b, j: (b, 0, j))
    par = lambda rows: pl.BlockSpec((rows, cb), lambda b, j: (0, j))
    return pl.pallas_call(
        _lru_kernel,
        grid=(bsz, c // cb),
        in_specs=[
            seq(n), seq(n_ctx), seq(n),
            par(conv_w.shape[0]), par(1),
            pl.BlockSpec((n_slabs,) + wg.shape[1:], lambda b, j: (j, 0, 0)),
            par(2),
        ],
        out_specs=seq(n),
        out_shape=jax.ShapeDtypeStruct((bsz, n, c), BF16),
        scratch_shapes=[
            pltpu.VMEM((n, cb), F32),
            pltpu.VMEM((n_ctx, cb), F32),
            pltpu.VMEM((2, n_slabs, lp, LANES), F32),
            pltpu.VMEM((2, n_slabs, lp, LANES), F32),
            pltpu.VMEM((2, n_slabs, cp, LANES), F32),
            pltpu.VMEM((2, n_slabs, cp, LANES), F32),
            pltpu.VMEM((2, n_slabs, lp, LANES), F32),
        ],
        compiler_params=_cparams(("parallel", "parallel")),
        name="rg_lru",
    )(ux, uxc, ug, conv_w, conv_b.reshape(1, c), wg, lam)


FFT_RADIX = 8
_FFT_BLOCK_OF = {0: (0, 0.0), 4: (4, 0.0), 2: (2, -1.0), 6: (2, 1.0), 1: (1, -1.0), 7: (1, 1.0), 3: (3, 1.0), 5: (3, -1.0)}


def _fourier_kernel(u_ref, m_ref, ccs_ref, wf_ref, bf_ref, o_ref, t_s, z_s, y_s):
    n = u_ref.shape[1]
    r = n // FFT_RADIX
    groups, gw, _ = wf_ref.shape
    c = math.sqrt(0.5)
    imag_blocks = (1, 2, 3)

    for g in range(groups):
        cols = slice(g * gw, (g + 1) * gw)
        x = [u_ref[0, p * r:(p + 1) * r, cols].astype(F32) for p in range(FFT_RADIX)]
        e0, e1, e2, e3 = x[0] + x[4], x[0] - x[4], x[2] + x[6], x[2] - x[6]
        o0, o1, o2, o3 = x[1] + x[5], x[1] - x[5], x[3] + x[7], x[3] - x[7]
        ee, oo = e0 + e2, o0 + o2
        p, m = c * (o1 - o3), c * (o1 + o3)
        t_re = {0: ee + oo, 4: ee - oo, 2: e0 - e2, 1: e1 + p, 3: e1 - p}
        t_im = {2: o0 - o2, 1: m + e3, 3: e3 - m}
        for blk in t_re:
            t_s[blk, 0, :, cols] = t_re[blk].astype(BF16)
            if blk in t_im:
                t_s[blk, 1, :, cols] = t_im[blk].astype(BF16)

        for k1 in range(FFT_RADIX):
            blk = _FFT_BLOCK_OF[k1][0]
            if blk in imag_blocks:
                z = jnp.dot(m_ref[k1], t_s[blk, :, :, cols].reshape(2 * r, gw), preferred_element_type=F32)
            else:
                z = jnp.dot(m_ref[k1, :, :r], t_s[blk, 0, :, cols], preferred_element_type=F32)
            z_s[k1, :, :, cols] = z.reshape(2, r, gw).astype(BF16)

        f = jnp.dot(z_s[:, 0, :, cols].reshape(n, gw), ccs_ref[0], preferred_element_type=F32)
        f = f + jnp.dot(z_s[:, 1, :, cols].reshape(n, gw), ccs_ref[1], preferred_element_type=F32)
        y = jnp.dot(f.astype(BF16), wf_ref[g], preferred_element_type=F32) + bf_ref[:, cols]
        for k1 in range(FFT_RADIX):
            for h in range(gw // LANES):
                slab = g * (gw // LANES) + h
                y_s[slab, pl.ds(k1, r, stride=FFT_RADIX), :] = y[k1 * r:(k1 + 1) * r, h * LANES:(h + 1) * LANES]

    for slab in range(y_s.shape[0]):
        o_ref[0, :, slab * LANES:(slab + 1) * LANES] = y_s[slab].astype(o_ref.dtype)


@functools.lru_cache(maxsize=None)
def _fourier_tables(n, gw):
    r = n // FFT_RADIX
    k = np.arange(FFT_RADIX)[:, None, None] + FFT_RADIX * np.arange(r)[None, :, None]
    ang = ((k * np.arange(r)[None, None, :]) % n) * (2.0 * math.pi / n)
    cn, sn = np.cos(ang) * n ** -0.5, np.sin(ang) * n ** -0.5
    sign = np.array([_FFT_BLOCK_OF[k1][1] for k1 in range(FFT_RADIX)])[:, None, None]
    m = np.concatenate([np.concatenate([cn, sign * sn], axis=2),
                        np.concatenate([-sn, sign * cn], axis=2)], axis=1)
    idx = np.arange(gw)
    ang_c = ((idx[:, None] * idx[None, :]) % gw) * (2.0 * math.pi / gw)
    ccs = np.stack([np.cos(ang_c), np.sin(ang_c)]) * gw ** -0.5
    return m.astype(np.float32), ccs.astype(np.float32)


def _fourier(uf, w_f, b_f):
    bsz, n, c = uf.shape
    groups, gw, _ = w_f.shape
    assert n % FFT_RADIX == 0 and gw % LANES == 0
    m, ccs = (jnp.asarray(t).astype(BF16) for t in _fourier_tables(n, gw))
    return pl.pallas_call(
        _fourier_kernel,
        grid=(bsz,),
        in_specs=[
            pl.BlockSpec((1, n, c), lambda b: (b, 0, 0)),
            _const_spec(m.shape),
            _const_spec(ccs.shape),
            _const_spec(w_f.shape),
            _const_spec((1, c)),
        ],
        out_specs=pl.BlockSpec((1, n, c), lambda b: (b, 0, 0)),
        out_shape=jax.ShapeDtypeStruct((bsz, n, c), BF16),
        scratch_shapes=[
            pltpu.VMEM((FFT_RADIX // 2 + 1, 2, n // FFT_RADIX, c), BF16),
            pltpu.VMEM((FFT_RADIX, 2, n // FFT_RADIX, c), BF16),
            pltpu.VMEM((c // LANES, n, LANES), F32),
        ],
        compiler_params=_cparams(("parallel",)),
        name="fourier",
    )(uf, m, ccs, w_f.astype(BF16), b_f.reshape(1, c))


def _out_proj_kernel(yf_ref, yl_ref, x_ref, prow_ref, pcol_ref, mod_ref, gpost_ref, gpre_ref, w_ref,
                     x1_ref, h2_ref):
    k = yf_ref.shape[2]
    pe = _tile_pos_embed(prow_ref, pcol_ref)
    assert sum(OUT_SUB_ROWS) == x_ref.shape[1]
    gain_post = mod_ref[0, 2:3, :] * gpost_ref[...]
    gain_pre = gpre_ref[...] * (1.0 + mod_ref[0, 4:5, :])
    for r0, n_rows in zip(itertools.accumulate((0,) + OUT_SUB_ROWS), OUT_SUB_ROWS):
        rs = slice(r0, r0 + n_rows)
        y = jnp.dot(yf_ref[0, rs, :], w_ref[:k, :], preferred_element_type=F32)
        y = y + jnp.dot(yl_ref[0, rs, :], w_ref[k:, :], preferred_element_type=F32)
        x1 = (x_ref[0, rs, :] + pe[rs, :]) + _rms_norm(y, gain_post)
        x1_ref[0, rs, :] = x1
        h2 = _rms_norm(x1, gain_pre) + mod_ref[0, 3:4, :]
        h2_ref[0, rs, :] = h2.astype(h2_ref.dtype)


def _out_proj(yf, yl, x, pe, mod3, g_post, g_pre, w_out):
    bsz, n, d = x.shape
    k = yf.shape[2]
    tm = sum(OUT_SUB_ROWS)
    act = pl.BlockSpec((1, tm, k), lambda i, b: (b, i, 0))
    row = pl.BlockSpec((1, tm, d), lambda i, b: (b, i, 0))
    return pl.pallas_call(
        _out_proj_kernel,
        grid=(n // tm, bsz),
        in_specs=[
            act, act, row,
            *_pos_embed_specs(pe, tm),
            pl.BlockSpec((1, N_MOD, d), lambda i, b: (b, 0, 0)),
            _const_spec((1, d)), _const_spec((1, d)),
            _const_spec(w_out.shape),
        ],
        out_specs=[row, row],
        out_shape=[jax.ShapeDtypeStruct((bsz, n, d), F32), jax.ShapeDtypeStruct((bsz, n, d), BF16)],
        compiler_params=_cparams(("parallel", "parallel")),
        name="out_proj",
    )(yf, yl, x, *pe, mod3, g_post.reshape(1, d), g_pre.reshape(1, d), w_out)


def _ffn_kernel(h_hbm, x1_hbm, wg_ref, wv_ref, cwg_ref, cwv_ref, cbg_ref, cbv_ref, wd_ref, mod_ref, gpost_ref,
                o_hbm, lhs_s, acc_s, xo_s, lhs_sem, x1_sem, out_sem, *, n_t, n_f):
    b, i, c = pl.program_id(0), pl.program_id(1), pl.program_id(2)
    tm = acc_s.shape[0]
    rows = lhs_s.shape[1]
    halo = (rows - tm) // 2
    t = b * n_t + i
    slot = t % 2
    last_tile = pl.num_programs(0) * n_t - 1

    def lhs_dma(bb, ii, sl, start):
        def run(src_row0, dst_row0, n_rows):
            cp = pltpu.make_async_copy(h_hbm.at[bb, pl.ds(src_row0, n_rows)],
                                       lhs_s.at[sl, pl.ds(dst_row0, n_rows)], lhs_sem.at[sl])
            if start:
                cp.start()
            else:
                cp.wait()

        pl.when(ii == 0)(lambda: run(0, halo, tm + halo))
        pl.when(ii == n_t - 1)(lambda: run((n_t - 1) * tm - halo, 0, tm + halo))
        if n_t > 2:
            pl.when(jnp.logical_and(ii > 0, ii < n_t - 1))(
                lambda: run(pl.multiple_of(ii * tm - halo, halo), 0, rows))

    def x1_copy():
        return pltpu.make_async_copy(x1_hbm.at[b, pl.ds(pl.multiple_of(i * tm, tm), tm)], xo_s, x1_sem.at[0])

    def out_copy(bb, ii):
        return pltpu.make_async_copy(xo_s, o_hbm.at[bb, pl.ds(pl.multiple_of(ii * tm, tm), tm)], out_sem.at[0])

    def conv_branch(w_ref, cw_ref, cb_ref):
        up = jnp.dot(lhs_s[slot], w_ref[...].astype(BF16), preferred_element_type=F32)
        left = FFN_CONV // 2
        y = cb_ref[...] + jnp.zeros((tm, up.shape[1]), F32)
        for k in range(FFN_CONV):
            off = k - left
            shifted = up if off == 0 else pltpu.roll(up, (-off) % rows, 0)
            y = y + shifted[halo:halo + tm, :] * cw_ref[k:k + 1, :]
        return y

    @pl.when(c == 0)
    def _():
        pl.when(t == 0)(lambda: lhs_dma(b, i, slot, True))
        lhs_dma(b, i, slot, False)
        zeros = jnp.zeros((halo, lhs_s.shape[2]), lhs_s.dtype)

        @pl.when(i == 0)
        def _():
            lhs_s[slot, 0:halo, :] = zeros

        @pl.when(i == n_t - 1)
        def _():
            lhs_s[slot, halo + tm:, :] = zeros

    @pl.when(c == 1)
    def _():
        nxt = i + 1 < n_t
        pl.when(t < last_tile)(
            lambda: lhs_dma(jnp.where(nxt, b, b + 1), jnp.where(nxt, i + 1, 0), 1 - slot, True))
        prv = i > 0
        pl.when(t > 0)(lambda: out_copy(jnp.where(prv, b, b - 1), jnp.where(prv, i - 1, n_t - 1)).wait())
        x1_copy().start()

    def gated_activation():
        g = conv_branch(wg_ref, cwg_ref, cbg_ref)
        v = conv_branch(wv_ref, cwv_ref, cbv_ref)
        return (jax.nn.gelu(g, approximate=True) * v).astype(BF16)

    @pl.when(c == 0)
    def _():
        acc_s[...] = jnp.dot(gated_activation(), wd_ref[...].astype(BF16), preferred_element_type=F32)

    @pl.when(jnp.logical_and(c > 0, c < n_f - 1))
    def _():
        acc_s[...] += jnp.dot(gated_activation(), wd_ref[...].astype(BF16), preferred_element_type=F32)

    @pl.when(c == n_f - 1)
    def _():
        act = gated_activation()
        wd = wd_ref[...].astype(BF16)
        x1_copy().wait()
        gain = mod_ref[0, 5:6, :] * gpost_ref[...]
        for r0 in range(0, tm, FFN_EPILOGUE_ROWS):
            rs = slice(r0, r0 + FFN_EPILOGUE_ROWS)
            y = acc_s[rs, :] + jnp.dot(act[rs, :], wd, preferred_element_type=F32)
            xo_s[rs, :] = xo_s[rs, :] + _rms_norm(y, gain)
        out_copy(b, i).start()
        pl.when(t == last_tile)(lambda: out_copy(b, i).wait())


def _conv_ffn(h2, x1, mod3, g_post, w_up, conv_w, conv_b, w_down):
    bsz, n, d = h2.shape
    d_ff = w_down.shape[0]
    tm, tf, halo = FFN_TM, FFN_TF, FFN_HALO
    nt, nf = n // tm, d_ff // tf
    assert nt >= 2 and nf >= 2 and n % tm == 0 and d_ff % tf == 0 and tm % FFN_EPILOGUE_ROWS == 0
    gcol = lambda r: pl.BlockSpec((r, tf), lambda b, i, c: (0, c))
    vcol = lambda r: pl.BlockSpec((r, tf), lambda b, i, c: (0, c + nf))
    any_spec = pl.BlockSpec(memory_space=pl.ANY)
    conv_b2 = conv_b.reshape(1, 2 * d_ff)
    rows = tm + 2 * halo
    return pl.pallas_call(
        functools.partial(_ffn_kernel, n_t=nt, n_f=nf),
        grid=(bsz, nt, nf),
        in_specs=[
            any_spec, any_spec,
            gcol(d), vcol(d),
            gcol(FFN_CONV), vcol(FFN_CONV),
            gcol(1), vcol(1),
            pl.BlockSpec((tf, d), lambda b, i, c: (c, 0)),
            pl.BlockSpec((1, N_MOD, d), lambda b, i, c: (b, 0, 0)),
            _const_spec((1, d)),
        ],
        out_specs=any_spec,
        out_shape=jax.ShapeDtypeStruct((bsz, n, d), F32),
        scratch_shapes=[
            pltpu.VMEM((2, rows, d), BF16),
            pltpu.VMEM((tm, d), F32),
            pltpu.VMEM((tm, d), F32),
            pltpu.SemaphoreType.DMA((2,)),
            pltpu.SemaphoreType.DMA((1,)),
            pltpu.SemaphoreType.DMA((1,)),
        ],
        compiler_params=_cparams(("arbitrary", "arbitrary", "arbitrary"), BIG_VMEM_LIMIT_BYTES),
        name="conv_ffn",
    )(h2, x1, w_up, w_up, conv_w, conv_w, conv_b2, conv_b2, w_down, mod3, g_post.reshape(1, d))


def _gate_weights(w_rec, b_rec, w_in, b_in):
    heads, hd, _ = w_rec.shape[1:]
    w = 0.5 * jnp.concatenate([w_rec[0], w_in[0], w_rec[1], w_in[1]], axis=-1)
    b = 0.5 * jnp.concatenate([v.reshape(heads, hd) for v in (b_rec[0], b_in[0], b_rec[1], b_in[1])], axis=-1)
    b_hi = b.astype(BF16)
    b_lo = (b - b_hi.astype(F32)).astype(BF16)
    pad = jnp.zeros((heads, hd - GATE_BIAS_ROWS, w.shape[-1]), BF16)
    return jnp.concatenate([w.astype(BF16), b_hi[:, None, :], b_lo[:, None, :], pad], axis=1)


def _grid_pos_tables(n_tokens, d_model):
    quarter = d_model // 4
    freqs = POS_BASE ** (-jnp.arange(quarter, dtype=F32) / quarter)

    def enc(count):
        ang = jnp.arange(count, dtype=F32)[:, None] * freqs[None, :]
        return jnp.concatenate([jnp.sin(ang), jnp.cos(ang)], axis=-1)

    return enc(n_tokens // GRID_W), enc(GRID_W)


def kernel(x, c, ctx, c_ctx, w_ada, b_ada, g_mix_pre, g_mix_post, g_ffn_pre, g_ffn_post,
           w_in, conv_lru_w, conv_lru_b, w_rec_gate, b_rec_gate, w_in_gate, b_in_gate,
           lru_lambda, w_fourier, b_fourier, w_out, w_up, conv_ffn_w, conv_ffn_b, w_down):
    bsz, n, d = x.shape
    d_fourier = w_fourier.shape[1] * w_fourier.shape[2]
    d_lru = conv_lru_w.shape[2]
    assert w_ada.shape[0] == 1, "single layer: the context stream is only read, never updated"
    assert bsz + 1 <= MOD_ROWS and d_lru // N_LRU_HEADS == LANES
    l = 0
    pe = _grid_pos_tables(n, d)

    cond = jnp.zeros((MOD_ROWS, d), F32).at[:bsz].set(c).at[bsz].set(c_ctx)
    mod3 = _modulation(cond, w_ada[l], b_ada[l]).reshape(MOD_ROWS, N_MOD, d)

    w_in_b = w_in[l].astype(BF16)
    x_cols = (d_fourier, d_fourier + d_lru)
    n_ctx = ctx.shape[1]
    (uxc,) = _in_proj(ctx.reshape(1, bsz * n_ctx, d), None, mod3, lambda b: bsz, g_mix_pre[l],
                      w_in_b, x_cols, ((0, d_lru),), (F32,), IN_TM, "in_proj_ctx")
    uxc = uxc.reshape(bsz, n_ctx, d_lru)
    uf, ux, ug = _in_proj(x, pe, mod3, lambda b: b, g_mix_pre[l], w_in_b, (0, w_in_b.shape[1]),
                          ((0, d_fourier), x_cols, (x_cols[1], x_cols[1] + d_lru)),
                          (BF16, F32, F32), IN_TM, "in_proj")

    y_lru = _rg_lru(ux, uxc, ug, conv_lru_w[l], conv_lru_b[l],
                    _gate_weights(w_rec_gate[l], b_rec_gate[l], w_in_gate[l], b_in_gate[l]), lru_lambda[l])
    y_fourier = _fourier(uf, w_fourier[l], b_fourier[l])

    x1, h2 = _out_proj(y_fourier, y_lru, x, pe, mod3, g_mix_post[l], g_ffn_pre[l], w_out[l].astype(BF16))
    return _conv_ffn(h2, x1, mod3, g_ffn_post[l], w_up[l], conv_ffn_w[l], conv_ffn_b[l], w_down[l])
```

```python
import functools
import itertools
import math

import numpy as np
import jax
import jax.numpy as jnp
from jax import lax
from jax.experimental import pallas as pl
from jax.experimental.pallas import tpu as pltpu

F32 = jnp.float32
BF16 = jnp.bfloat16

EPS = 1e-6
POS_BASE = 10000.0
GRID_W = 64
N_MOD = 6
N_LRU_HEADS = 8
LRU_CONV = 4
LRU_C = 8.0
FFN_CONV = 3

LANES = 128
SUBLANES = 8
VMEM_BYTES = 64 * 1024 * 1024
VMEM_LIMIT_BYTES = 60000 * 1024
BIG_VMEM_LIMIT_BYTES = VMEM_BYTES - 2 * 1024 * 1024

MOD_ROWS = 16
MOD_TN = 1024
IN_TM = 1024
IN_SUB_TILES = 4
OUT_SUB_ROWS = (256, 256)
LRU_CB = 512
LRU_ROW_CHUNK = 512
SCAN_UNROLL = 4
GELU_C0 = math.sqrt(2.0 / math.pi)
GELU_C1 = 0.044715 * GELU_C0
GATE_BIAS_ROWS = 2
FFN_TM = 1024
FFN_TF = 512
FFN_HALO = 16
FFN_EPILOGUE_ROWS = 256


def _cparams(sem, vmem_limit_bytes=VMEM_LIMIT_BYTES):
    return pltpu.CompilerParams(dimension_semantics=sem, vmem_limit_bytes=vmem_limit_bytes)


def _const_spec(shape):
    nd = len(shape)
    return pl.BlockSpec(shape, lambda *_: (0,) * nd, pipeline_mode=pl.Buffered(1))


def _mod_kernel(c_ref, w_ref, b_ref, o_ref):
    cv = c_ref[...]
    s = (cv * jax.nn.sigmoid(cv)).astype(BF16)
    o_ref[...] = jnp.dot(s, w_ref[...].astype(BF16), preferred_element_type=F32) + b_ref[...]


def _modulation(cond, w_ada, b_ada):
    d = cond.shape[1]
    n = w_ada.shape[1]
    tn = MOD_TN
    return pl.pallas_call(
        _mod_kernel,
        grid=(n // tn,),
        in_specs=[
            pl.BlockSpec((MOD_ROWS, d), lambda j: (0, 0)),
            pl.BlockSpec((d, tn), lambda j: (0, j)),
            pl.BlockSpec((1, tn), lambda j: (0, j)),
        ],
        out_specs=pl.BlockSpec((MOD_ROWS, tn), lambda j: (0, j)),
        out_shape=jax.ShapeDtypeStruct((MOD_ROWS, n), F32),
        compiler_params=_cparams(("parallel",)),
        name="modulation",
    )(cond, w_ada, b_ada.reshape(1, n))


def _rms_norm(v, g):
    ms = jnp.mean(v * v, axis=-1, keepdims=True)
    return v * lax.rsqrt(ms + EPS) * g


def _tile_pos_embed(prow_ref, pcol_ref):
    rows = jnp.concatenate([jnp.broadcast_to(prow_ref[r:r + 1, :], pcol_ref.shape)
                            for r in range(prow_ref.shape[0])], axis=0)
    cols = jnp.concatenate([pcol_ref[...]] * prow_ref.shape[0], axis=0)
    return jnp.concatenate([rows, cols], axis=1)


def _in_proj_kernel(*refs, has_pe, col_splits):
    if has_pe:
        x_ref, prow_ref, pcol_ref, mod_ref, g_ref, w_ref = refs[:6]
        out_refs = refs[6:]
        pe = _tile_pos_embed(prow_ref, pcol_ref)
    else:
        x_ref, mod_ref, g_ref, w_ref = refs[:4]
        out_refs = refs[4:]
        pe = None
    n_rows = x_ref.shape[1]
    sub = n_rows // IN_SUB_TILES
    gain = g_ref[...] * (1.0 + mod_ref[0, 1:2, :])
    for r0 in range(0, n_rows, sub):
        rs = slice(r0, r0 + sub)
        xp = x_ref[0, rs, :] if pe is None else x_ref[0, rs, :] + pe[rs, :]
        h = _rms_norm(xp, gain) + mod_ref[0, 0:1, :]
        hb = h.astype(BF16)
        for o_ref, (c0, c1) in zip(out_refs, col_splits):
            o_ref[0, rs, :] = jnp.dot(hb, w_ref[:, c0:c1], preferred_element_type=F32).astype(o_ref.dtype)


def _pos_embed_specs(pe, tm):
    prow, pcol = pe
    assert tm % GRID_W == 0
    return [pl.BlockSpec((tm // GRID_W, prow.shape[1]), lambda i, b: (i, 0)), _const_spec(pcol.shape)]


def _in_proj(x, pe, mod3, mod_row, g, w, w_cols, col_splits, out_dtypes, tm, name):
    bsz, n, d = x.shape
    w_width = w_cols[1] - w_cols[0]
    assert w_cols[0] % w_width == 0
    has_pe = pe is not None
    in_specs = [pl.BlockSpec((1, tm, d), lambda i, b: (b, i, 0))]
    args = [x]
    if has_pe:
        in_specs += _pos_embed_specs(pe, tm)
        args += list(pe)
    in_specs += [
        pl.BlockSpec((1, N_MOD, d), lambda i, b: (mod_row(b), 0, 0)),
        _const_spec((1, d)),
        pl.BlockSpec((w.shape[0], w_width), lambda i, b: (0, w_cols[0] // w_width), pipeline_mode=pl.Buffered(1)),
    ]
    args += [mod3, g.reshape(1, d), w]
    out_specs = [pl.BlockSpec((1, tm, c1 - c0), lambda i, b: (b, i, 0)) for c0, c1 in col_splits]
    out_shape = [jax.ShapeDtypeStruct((bsz, n, c1 - c0), dt) for (c0, c1), dt in zip(col_splits, out_dtypes)]
    return pl.pallas_call(
        functools.partial(_in_proj_kernel, has_pe=has_pe, col_splits=col_splits),
        grid=(n // tm, bsz),
        in_specs=in_specs,
        out_specs=out_specs,
        out_shape=out_shape,
        compiler_params=_cparams(("parallel", "parallel")),
        name=name,
    )(*args)


def _scan_pitch(n):
    m = -(-n // SUBLANES)
    while m % 8 != 4:
        m += 1
    return m


def _depthwise_conv_rows(x, w_ref, b_ref, left):
    n = x.shape[0]
    k_w = w_ref.shape[0]
    row = lax.broadcasted_iota(jnp.int32, x.shape, 0)
    y = b_ref[...] + jnp.zeros_like(x)
    for k in range(k_w):
        off = k - left
        if off == 0:
            xs = x
        elif off < 0:
            xs = jnp.where(row >= -off, pltpu.roll(x, -off, 0), 0.0)
        else:
            xs = jnp.where(row < n - off, pltpu.roll(x, n - off, 0), 0.0)
        y = y + xs * w_ref[k:k + 1, :]
    return y


def _log_sigmoid(z):
    return jnp.minimum(z, 0.0) - jnp.log1p(jnp.exp(-jnp.abs(z)))


def _lru_gates(xc_ref, n, row_chunk, wg_ref, quarter_c_ls, a_ref, b_ref, n_slabs):
    bias_cols = (lax.broadcasted_iota(jnp.int32, (row_chunk, LANES), 1) < GATE_BIAS_ROWS).astype(BF16)
    for r0 in range(0, n, row_chunk):
        for s in range(n_slabs):
            lanes = slice(s * LANES, (s + 1) * LANES)
            xh = xc_ref[r0:r0 + row_chunk, lanes]
            lhs = jnp.concatenate([xh.astype(BF16), bias_cols], axis=1)
            g = jnp.dot(lhs, wg_ref[s], preferred_element_type=F32)
            for d in range(2):
                tr = jnp.tanh(g[:, (2 * d) * LANES:(2 * d + 1) * LANES])
                ti = jnp.tanh(g[:, (2 * d + 1) * LANES:(2 * d + 2) * LANES])
                hl = -quarter_c_ls[d:d + 1, lanes]
                u = hl + hl * tr
                t = jnp.tanh(u)
                q = pl.reciprocal(1.0 + t, full_range=False)
                root = jnp.where(t > 0.0, t * lax.rsqrt(t), 0.0)
                a_ref[d, s, r0:r0 + row_chunk, :] = jnp.exp2((-2.0 / math.log(2.0)) * u)
                b_ref[d, s, r0:r0 + row_chunk, :] = (root * q) * ((1.0 + ti) * xh)


def _lru_scan(a_ref, b_ref, n, h0, n_slabs, h_ref=None):
    m = _scan_pitch(n)
    pairs = [(d, s) for d in range(2) for s in range(n_slabs)]
    sub = lax.broadcasted_iota(jnp.int32, (SUBLANES, LANES), 0)

    def rows_of(d, j):
        return pl.ds(j if d == 0 else m - 1 - j, SUBLANES, stride=m)

    def summary_body(j, carry):
        out = []
        for (d, s), (h, p) in zip(pairs, carry):
            a = a_ref[d, s, rows_of(d, j), :]
            out.append((a * h + b_ref[d, s, rows_of(d, j), :], a * p))
        return tuple(out)

    init = tuple((jnp.zeros((SUBLANES, LANES), F32), jnp.ones((SUBLANES, LANES), F32)) for _ in pairs)
    local = lax.fori_loop(0, m, summary_body, init, unroll=SCAN_UNROLL)

    carries, finals = [], []
    for (d, s), (hfin, ptot) in zip(pairs, local):
        first, last, shift = (0, SUBLANES - 1, 1) if d == 0 else (SUBLANES - 1, 0, SUBLANES - 1)
        hin = jnp.where(sub == first, jnp.broadcast_to(h0[d][s], (SUBLANES, LANES)), 0.0)
        for step in range(1, SUBLANES):
            k = first + step if d == 0 else first - step
            nxt = pltpu.roll(hfin + ptot * hin, shift, 0)
            hin = jnp.where(sub == k, nxt, hin)
        carries.append(hin)
        finals.append((hfin + ptot * hin)[last:last + 1, :])

    if h_ref is not None:
        def state_body(j, carry):
            out = []
            for (d, s), h in zip(pairs, carry):
                h = a_ref[d, s, rows_of(d, j), :] * h + b_ref[d, s, rows_of(d, j), :]
                h_ref[d, s, rows_of(d, j), :] = h
                out.append(h)
            return tuple(out)

        lax.fori_loop(0, m, state_body, tuple(carries), unroll=SCAN_UNROLL)

    return [[finals[d * n_slabs + s] for s in range(n_slabs)] for d in range(2)]


def _lru_kernel(ux_ref, uxc_ref, ug_ref, cw_ref, cb_ref, wg_ref, lam_ref, o_ref,
                xc_s, xcc_s, a_s, b_s, ac_s, bc_s, h_s):
    n = ux_ref.shape[1]
    n_ctx = uxc_ref.shape[1]
    n_slabs = ux_ref.shape[2] // LANES
    ls = (0.25 * LRU_C) * _log_sigmoid(lam_ref[...])

    for ref, rows in ((a_s, n), (ac_s, n_ctx)):
        ref[:, :, rows:, :] = jnp.ones((2, n_slabs, ref.shape[2] - rows, LANES), F32)
    for ref, rows in ((b_s, n), (bc_s, n_ctx)):
        ref[:, :, rows:, :] = jnp.zeros((2, n_slabs, ref.shape[2] - rows, LANES), F32)

    left = LRU_CONV // 2
    for s in range(n_slabs):
        lanes = slice(s * LANES, (s + 1) * LANES)
        taps = (cw_ref.at[:, lanes], cb_ref.at[:, lanes], left)
        xcc_s[:, lanes] = _depthwise_conv_rows(uxc_ref[0, :, lanes], *taps)
        xc_s[:, lanes] = _depthwise_conv_rows(ux_ref[0, :, lanes], *taps)

    _lru_gates(xcc_s, n_ctx, n_ctx, wg_ref, ls, ac_s, bc_s, n_slabs)
    _lru_gates(xc_s, n, LRU_ROW_CHUNK, wg_ref, ls, a_s, b_s, n_slabs)

    zero = jnp.zeros((1, LANES), F32)
    h0 = _lru_scan(ac_s, bc_s, n_ctx, [[zero] * n_slabs] * 2, n_slabs)
    _lru_scan(a_s, b_s, n, h0, n_slabs, h_s)

    for r0 in range(0, n, LRU_ROW_CHUNK):
        rows = slice(r0, r0 + LRU_ROW_CHUNK)
        for s in range(n_slabs):
            lanes = slice(s * LANES, (s + 1) * LANES)
            hsum = h_s[0, s, rows, :] + h_s[1, s, rows, :]
            ug = ug_ref[0, rows, lanes]
            th = jnp.tanh(ug * (GELU_C0 + GELU_C1 * (ug * ug)))
            o_ref[0, rows, lanes] = ((hsum * ug) * (1.0 + th)).astype(o_ref.dtype)


def _rg_lru(ux, uxc, ug, conv_w, conv_b, wg, lam):
    bsz, n, c = ux.shape
    n_ctx = uxc.shape[1]
    cb = LRU_CB
    n_slabs = cb // LANES
    lp = SUBLANES * _scan_pitch(n)
    cp = SUBLANES * _scan_pitch(n_ctx)
    seq = lambda rows: pl.BlockSpec((1, rows, cb), lambda b, j: (b, 0, j))
    par = lambda rows: pl.BlockSpec((rows, cb), lambda b, j: (0, j))
    return pl.pallas_call(
        _lru_kernel,
        grid=(bsz, c // cb),
        in_specs=[
            seq(n), seq(n_ctx), seq(n),
            par(conv_w.shape[0]), par(1),
            pl.BlockSpec((n_slabs,) + wg.shape[1:], lambda b, j: (j, 0, 0)),
            par(2),
        ],
        out_specs=seq(n),
        out_shape=jax.ShapeDtypeStruct((bsz, n, c), BF16),
        scratch_shapes=[
            pltpu.VMEM((n, cb), F32),
            pltpu.VMEM((n_ctx, cb), F32),
            pltpu.VMEM((2, n_slabs, lp, LANES), F32),
            pltpu.VMEM((2, n_slabs, lp, LANES), F32),
            pltpu.VMEM((2, n_slabs, cp, LANES), F32),
            pltpu.VMEM((2, n_slabs, cp, LANES), F32),
            pltpu.VMEM((2, n_slabs, lp, LANES), F32),
        ],
        compiler_params=_cparams(("parallel", "parallel")),
        name="rg_lru",
    )(ux, uxc, ug, conv_w, conv_b.reshape(1, c), wg, lam)


FFT_RADIX = 8
_FFT_BLOCK_OF = {0: (0, 0.0), 4: (4, 0.0), 2: (2, -1.0), 6: (2, 1.0), 1: (1, -1.0), 7: (1, 1.0), 3: (3, 1.0), 5: (3, -1.0)}


def _fourier_kernel(u_ref, m_ref, ccs_ref, wf_ref, bf_ref, o_ref, t_s, z_s, y_s):
    n = u_ref.shape[1]
    r = n // FFT_RADIX
    groups, gw, _ = wf_ref.shape
    c = math.sqrt(0.5)
    imag_blocks = (1, 2, 3)

    for g in range(groups):
        cols = slice(g * gw, (g + 1) * gw)
        x = [u_ref[0, p * r:(p + 1) * r, cols].astype(F32) for p in range(FFT_RADIX)]
        e0, e1, e2, e3 = x[0] + x[4], x[0] - x[4], x[2] + x[6], x[2] - x[6]
        o0, o1, o2, o3 = x[1] + x[5], x[1] - x[5], x[3] + x[7], x[3] - x[7]
        ee, oo = e0 + e2, o0 + o2
        p, m = c * (o1 - o3), c * (o1 + o3)
        t_re = {0: ee + oo, 4: ee - oo, 2: e0 - e2, 1: e1 + p, 3: e1 - p}
        t_im = {2: o0 - o2, 1: m + e3, 3: e3 - m}
        for blk in t_re:
            t_s[blk, 0, :, cols] = t_re[blk].astype(BF16)
            if blk in t_im:
                t_s[blk, 1, :, cols] = t_im[blk].astype(BF16)

        for k1 in range(FFT_RADIX):
            blk = _FFT_BLOCK_OF[k1][0]
            if blk in imag_blocks:
                z = jnp.dot(m_ref[k1], t_s[blk, :, :, cols].reshape(2 * r, gw), preferred_element_type=F32)
            else:
                z = jnp.dot(m_ref[k1, :, :r], t_s[blk, 0, :, cols], preferred_element_type=F32)
            z_s[k1, :, :, cols] = z.reshape(2, r, gw).astype(BF16)

        f = jnp.dot(z_s[:, 0, :, cols].reshape(n, gw), ccs_ref[0], preferred_element_type=F32)
        f = f + jnp.dot(z_s[:, 1, :, cols].reshape(n, gw), ccs_ref[1], preferred_element_type=F32)
        y = jnp.dot(f.astype(BF16), wf_ref[g], preferred_element_type=F32) + bf_ref[:, cols]
        for k1 in range(FFT_RADIX):
            for h in range(gw // LANES):
                slab = g * (gw // LANES) + h
                y_s[slab, pl.ds(k1, r, stride=FFT_RADIX), :] = y[k1 * r:(k1 + 1) * r, h * LANES:(h + 1) * LANES]

    for slab in range(y_s.shape[0]):
        o_ref[0, :, slab * LANES:(slab + 1) * LANES] = y_s[slab].astype(o_ref.dtype)


@functools.lru_cache(maxsize=None)
def _fourier_tables(n, gw):
    r = n // FFT_RADIX
    k = np.arange(FFT_RADIX)[:, None, None] + FFT_RADIX * np.arange(r)[None, :, None]
    ang = ((k * np.arange(r)[None, None, :]) % n) * (2.0 * math.pi / n)
    cn, sn = np.cos(ang) * n ** -0.5, np.sin(ang) * n ** -0.5
    sign = np.array([_FFT_BLOCK_OF[k1][1] for k1 in range(FFT_RADIX)])[:, None, None]
    m = np.concatenate([np.concatenate([cn, sign * sn], axis=2),
                        np.concatenate([-sn, sign * cn], axis=2)], axis=1)
    idx = np.arange(gw)
    ang_c = ((idx[:, None] * idx[None, :]) % gw) * (2.0 * math.pi / gw)
    ccs = np.stack([np.cos(ang_c), np.sin(ang_c)]) * gw ** -0.5
    return m.astype(np.float32), ccs.astype(np.float32)


def _fourier(uf, w_f, b_f):
    bsz, n, c = uf.shape
    groups, gw, _ = w_f.shape
    assert n % FFT_RADIX == 0 and gw % LANES == 0
    m, ccs = (jnp.asarray(t).astype(BF16) for t in _fourier_tables(n, gw))
    return pl.pallas_call(
        _fourier_kernel,
        grid=(bsz,),
        in_specs=[
            pl.BlockSpec((1, n, c), lambda b: (b, 0, 0)),
            _const_spec(m.shape),
            _const_spec(ccs.shape),
            _const_spec(w_f.shape),
            _const_spec((1, c)),
        ],
        out_specs=pl.BlockSpec((1, n, c), lambda b: (b, 0, 0)),
        out_shape=jax.ShapeDtypeStruct((bsz, n, c), BF16),
        scratch_shapes=[
            pltpu.VMEM((FFT_RADIX // 2 + 1, 2, n // FFT_RADIX, c), BF16),
            pltpu.VMEM((FFT_RADIX, 2, n // FFT_RADIX, c), BF16),
            pltpu.VMEM((c // LANES, n, LANES), F32),
        ],
        compiler_params=_cparams(("parallel",)),
        name="fourier",
    )(uf, m, ccs, w_f.astype(BF16), b_f.reshape(1, c))


def _out_proj_kernel(yf_ref, yl_ref, x_ref, prow_ref, pcol_ref, mod_ref, gpost_ref, gpre_ref, w_ref,
                     x1_ref, h2_ref):
    k = yf_ref.shape[2]
    pe = _tile_pos_embed(prow_ref, pcol_ref)
    assert sum(OUT_SUB_ROWS) == x_ref.shape[1]
    gain_post = mod_ref[0, 2:3, :] * gpost_ref[...]
    gain_pre = gpre_ref[...] * (1.0 + mod_ref[0, 4:5, :])
    for r0, n_rows in zip(itertools.accumulate((0,) + OUT_SUB_ROWS), OUT_SUB_ROWS):
        rs = slice(r0, r0 + n_rows)
        y = jnp.dot(yf_ref[0, rs, :], w_ref[:k, :], preferred_element_type=F32)
        y = y + jnp.dot(yl_ref[0, rs, :], w_ref[k:, :], preferred_element_type=F32)
        x1 = (x_ref[0, rs, :] + pe[rs, :]) + _rms_norm(y, gain_post)
        x1_ref[0, rs, :] = x1
        h2 = _rms_norm(x1, gain_pre) + mod_ref[0, 3:4, :]
        h2_ref[0, rs, :] = h2.astype(h2_ref.dtype)


def _out_proj(yf, yl, x, pe, mod3, g_post, g_pre, w_out):
    bsz, n, d = x.shape
    k = yf.shape[2]
    tm = sum(OUT_SUB_ROWS)
    act = pl.BlockSpec((1, tm, k), lambda i, b: (b, i, 0))
    row = pl.BlockSpec((1, tm, d), lambda i, b: (b, i, 0))
    return pl.pallas_call(
        _out_proj_kernel,
        grid=(n // tm, bsz),
        in_specs=[
            act, act, row,
            *_pos_embed_specs(pe, tm),
            pl.BlockSpec((1, N_MOD, d), lambda i, b: (b, 0, 0)),
            _const_spec((1, d)), _const_spec((1, d)),
            _const_spec(w_out.shape),
        ],
        out_specs=[row, row],
        out_shape=[jax.ShapeDtypeStruct((bsz, n, d), F32), jax.ShapeDtypeStruct((bsz, n, d), BF16)],
        compiler_params=_cparams(("parallel", "parallel")),
        name="out_proj",
    )(yf, yl, x, *pe, mod3, g_post.reshape(1, d), g_pre.reshape(1, d), w_out)


def _ffn_kernel(h_hbm, x1_hbm, wg_ref, wv_ref, cwg_ref, cwv_ref, cbg_ref, cbv_ref, wd_ref, mod_ref, gpost_ref,
                o_hbm, lhs_s, acc_s, xo_s, lhs_sem, x1_sem, out_sem, *, n_t, n_f):
    b, i, c = pl.program_id(0), pl.program_id(1), pl.program_id(2)
    tm = acc_s.shape[0]
    rows = lhs_s.shape[1]
    halo = (rows - tm) // 2
    t = b * n_t + i
    slot = t % 2
    last_tile = pl.num_programs(0) * n_t - 1

    def lhs_dma(bb, ii, sl, start):
        def run(src_row0, dst_row0, n_rows):
            cp = pltpu.make_async_copy(h_hbm.at[bb, pl.ds(src_row0, n_rows)],
                                       lhs_s.at[sl, pl.ds(dst_row0, n_rows)], lhs_sem.at[sl])
            if start:
                cp.start()
            else:
                cp.wait()

        pl.when(ii == 0)(lambda: run(0, halo, tm + halo))
        pl.when(ii == n_t - 1)(lambda: run((n_t - 1) * tm - halo, 0, tm + halo))
        if n_t > 2:
            pl.when(jnp.logical_and(ii > 0, ii < n_t - 1))(
                lambda: run(pl.multiple_of(ii * tm - halo, halo), 0, rows))

    def x1_copy():
        return pltpu.make_async_copy(x1_hbm.at[b, pl.ds(pl.multiple_of(i * tm, tm), tm)], xo_s, x1_sem.at[0])

    def out_copy(bb, ii):
        return pltpu.make_async_copy(xo_s, o_hbm.at[bb, pl.ds(pl.multiple_of(ii * tm, tm), tm)], out_sem.at[0])

    def conv_branch(w_ref, cw_ref, cb_ref):
        up = jnp.dot(lhs_s[slot], w_ref[...].astype(BF16), preferred_element_type=F32)
        left = FFN_CONV // 2
        y = cb_ref[...] + jnp.zeros((tm, up.shape[1]), F32)
        for k in range(FFN_CONV):
            off = k - left
            shifted = up if off == 0 else pltpu.roll(up, (-off) % rows, 0)
            y = y + shifted[halo:halo + tm, :] * cw_ref[k:k + 1, :]
        return y

    @pl.when(c == 0)
    def _():
        pl.when(t == 0)(lambda: lhs_dma(b, i, slot, True))
        lhs_dma(b, i, slot, False)
        zeros = jnp.zeros((halo, lhs_s.shape[2]), lhs_s.dtype)

        @pl.when(i == 0)
        def _():
            lhs_s[slot, 0:halo, :] = zeros

        @pl.when(i == n_t - 1)
        def _():
            lhs_s[slot, halo + tm:, :] = zeros

    @pl.when(c == 1)
    def _():
        nxt = i + 1 < n_t
        pl.when(t < last_tile)(
            lambda: lhs_dma(jnp.where(nxt, b, b + 1), jnp.where(nxt, i + 1, 0), 1 - slot, True))
        prv = i > 0
        pl.when(t > 0)(lambda: out_copy(jnp.where(prv, b, b - 1), jnp.where(prv, i - 1, n_t - 1)).wait())
        x1_copy().start()

    def gated_activation():
        g = conv_branch(wg_ref, cwg_ref, cbg_ref)
        v = conv_branch(wv_ref, cwv_ref, cbv_ref)
        return (jax.nn.gelu(g, approximate=True) * v).astype(BF16)

    @pl.when(c == 0)
    def _():
        acc_s[...] = jnp.dot(gated_activation(), wd_ref[...].astype(BF16), preferred_element_type=F32)

    @pl.when(jnp.logical_and(c > 0, c < n_f - 1))
    def _():
        acc_s[...] += jnp.dot(gated_activation(), wd_ref[...].astype(BF16), preferred_element_type=F32)

    pl.when(c == n_f - 1)(lambda: x1_copy().wait())

    @pl.when(c == n_f - 1)
    def _():
        act = gated_activation()
        wd = wd_ref[...].astype(BF16)
        gain = mod_ref[0, 5:6, :] * gpost_ref[...]
        for r0 in range(0, tm, FFN_EPILOGUE_ROWS):
            rs = slice(r0, r0 + FFN_EPILOGUE_ROWS)
            y = acc_s[rs, :] + jnp.dot(act[rs, :], wd, preferred_element_type=F32)
            xo_s[rs, :] = xo_s[rs, :] + _rms_norm(y, gain)
        out_copy(b, i).start()
        pl.when(t == last_tile)(lambda: out_copy(b, i).wait())


def _conv_ffn(h2, x1, mod3, g_post, w_up, conv_w, conv_b, w_down):
    bsz, n, d = h2.shape
    d_ff = w_down.shape[0]
    tm, tf, halo = FFN_TM, FFN_TF, FFN_HALO
    nt, nf = n // tm, d_ff // tf
    assert nt >= 2 and nf >= 2 and n % tm == 0 and d_ff % tf == 0 and tm % FFN_EPILOGUE_ROWS == 0
    gcol = lambda r: pl.BlockSpec((r, tf), lambda b, i, c: (0, c))
    vcol = lambda r: pl.BlockSpec((r, tf), lambda b, i, c: (0, c + nf))
    any_spec = pl.BlockSpec(memory_space=pl.ANY)
    conv_b2 = conv_b.reshape(1, 2 * d_ff)
    rows = tm + 2 * halo
    return pl.pallas_call(
        functools.partial(_ffn_kernel, n_t=nt, n_f=nf),
        grid=(bsz, nt, nf),
        in_specs=[
            any_spec, any_spec,
            gcol(d), vcol(d),
            gcol(FFN_CONV), vcol(FFN_CONV),
            gcol(1), vcol(1),
            pl.BlockSpec((tf, d), lambda b, i, c: (c, 0)),
            pl.BlockSpec((1, N_MOD, d), lambda b, i, c: (b, 0, 0)),
            _const_spec((1, d)),
        ],
        out_specs=any_spec,
        out_shape=jax.ShapeDtypeStruct((bsz, n, d), F32),
        scratch_shapes=[
            pltpu.VMEM((2, rows, d), BF16),
            pltpu.VMEM((tm, d), F32),
            pltpu.VMEM((tm, d), F32),
            pltpu.SemaphoreType.DMA((2,)),
            pltpu.SemaphoreType.DMA((1,)),
            pltpu.SemaphoreType.DMA((1,)),
        ],
        compiler_params=_cparams(("arbitrary", "arbitrary", "arbitrary"), BIG_VMEM_LIMIT_BYTES),
        name="conv_ffn",
    )(h2, x1, w_up, w_up, conv_w, conv_w, conv_b2, conv_b2, w_down, mod3, g_post.reshape(1, d))


def _gate_weights(w_rec, b_rec, w_in, b_in):
    heads, hd, _ = w_rec.shape[1:]
    w = 0.5 * jnp.concatenate([w_rec[0], w_in[0], w_rec[1], w_in[1]], axis=-1)
    b = 0.5 * jnp.concatenate([v.reshape(heads, hd) for v in (b_rec[0], b_in[0], b_rec[1], b_in[1])], axis=-1)
    b_hi = b.astype(BF16)
    b_lo = (b - b_hi.astype(F32)).astype(BF16)
    pad = jnp.zeros((heads, hd - GATE_BIAS_ROWS, w.shape[-1]), BF16)
    return jnp.concatenate([w.astype(BF16), b_hi[:, None, :], b_lo[:, None, :], pad], axis=1)


def _grid_pos_tables(n_tokens, d_model):
    quarter = d_model // 4
    freqs = POS_BASE ** (-jnp.arange(quarter, dtype=F32) / quarter)

    def enc(count):
        ang = jnp.arange(count, dtype=F32)[:, None] * freqs[None, :]
        return jnp.concatenate([jnp.sin(ang), jnp.cos(ang)], axis=-1)

    return enc(n_tokens // GRID_W), enc(GRID_W)


def kernel(x, c, ctx, c_ctx, w_ada, b_ada, g_mix_pre, g_mix_post, g_ffn_pre, g_ffn_post,
           w_in, conv_lru_w, conv_lru_b, w_rec_gate, b_rec_gate, w_in_gate, b_in_gate,
           lru_lambda, w_fourier, b_fourier, w_out, w_up, conv_ffn_w, conv_ffn_b, w_down):
    bsz, n, d = x.shape
    d_fourier = w_fourier.shape[1] * w_fourier.shape[2]
    d_lru = conv_lru_w.shape[2]
    assert w_ada.shape[0] == 1, "single layer: the context stream is only read, never updated"
    assert bsz + 1 <= MOD_ROWS and d_lru // N_LRU_HEADS == LANES
    l = 0
    pe = _grid_pos_tables(n, d)

    cond = jnp.zeros((MOD_ROWS, d), F32).at[:bsz].set(c).at[bsz].set(c_ctx)
    mod3 = _modulation(cond, w_ada[l], b_ada[l]).reshape(MOD_ROWS, N_MOD, d)

    w_in_b = w_in[l].astype(BF16)
    x_cols = (d_fourier, d_fourier + d_lru)
    n_ctx = ctx.shape[1]
    (uxc,) = _in_proj(ctx.reshape(1, bsz * n_ctx, d), None, mod3, lambda b: bsz, g_mix_pre[l],
                      w_in_b, x_cols, ((0, d_lru),), (F32,), IN_TM, "in_proj_ctx")
    uxc = uxc.reshape(bsz, n_ctx, d_lru)
    uf, ux, ug = _in_proj(x, pe, mod3, lambda b: b, g_mix_pre[l], w_in_b, (0, w_in_b.shape[1]),
                          ((0, d_fourier), x_cols, (x_cols[1], x_cols[1] + d_lru)),
                          (BF16, F32, F32), IN_TM, "in_proj")

    y_lru = _rg_lru(ux, uxc, ug, conv_lru_w[l], conv_lru_b[l],
                    _gate_weights(w_rec_gate[l], b_rec_gate[l], w_in_gate[l], b_in_gate[l]), lru_lambda[l])
    y_fourier = _fourier(uf, w_fourier[l], b_fourier[l])

    branch_scale = jnp.where(jnp.arange(w_out.shape[1]) < d_fourier, 1.0, 0.5)[:, None]
    x1, h2 = _out_proj(y_fourier, y_lru, x, pe, mod3, g_mix_post[l], g_ffn_pre[l],
                       (w_out[l] * branch_scale).astype(BF16))
    return _conv_ffn(h2, x1, mod3, g_ffn_post[l], w_up[l], conv_ffn_w[l], conv_ffn_b[l], w_down[l])
```
